```python
import math
import jax, jax.numpy as jnp
from jax import lax
import numpy as np

D_MODEL = 1024
BATCH = 16
SEQ = 4096
DEPTH = 2

D_PLE = 256
D_MIX = D_MODEL
CONV_CH = D_MIX // 4
CONV_WIDTH = 31
DIFF_HEADS = 4
DIFF_HD = 32
DIFF_VD = 2 * DIFF_HD
DIFF_W = DIFF_HEADS * DIFF_VD
DIFF_QK = DIFF_HEADS * 2 * DIFF_HD
SSD_HEADS = 8
SSD_HD = 64
SSD_W = SSD_HEADS * SSD_HD
SSD_GROUPS = 2
SSD_STATE = 128
SSD_CONV = 4
SSD_CHUNK = 128
SSD_XBC = SSD_W + 2 * SSD_GROUPS * SSD_STATE
SPLIT_SIZES = (2 * CONV_CH, DIFF_QK, DIFF_QK, DIFF_W, SSD_W, SSD_XBC, SSD_HEADS)
D_IN = 2 * CONV_CH + 2 * DIFF_QK + DIFF_W + SSD_W + SSD_XBC + SSD_HEADS
ROPE_THETA = 10000.0
Q_BLOCK = 128
D_FF = 2816
FFN_CONV = 3
EPS = 1e-6

kernel_name = "hybrid_conv_diffattn_ssd_block"


def rms_norm(x, g):
    xf = x.astype(jnp.float32)
    y = xf * lax.rsqrt(jnp.mean(xf * xf, axis=-1, keepdims=True) + EPS)
    return (y * g.astype(jnp.float32)).astype(x.dtype)


def layer_norm(x, g, b):
    xf = x.astype(jnp.float32)
    mu = jnp.mean(xf, axis=-1, keepdims=True)
    xc = xf - mu
    y = xc * lax.rsqrt(jnp.mean(xc * xc, axis=-1, keepdims=True) + EPS)
    return (y * g.astype(jnp.float32) + b.astype(jnp.float32)).astype(x.dtype)


def causal_dwconv(x, w, b):
    k = w.shape[0]
    y = lax.conv_general_dilated(
        x, w[:, None, :].astype(x.dtype), window_strides=(1,), padding=[(k - 1, 0)],
        dimension_numbers=("NWC", "WIO", "NWC"), feature_group_count=x.shape[-1])
    return y + b.astype(x.dtype)


def rope_tables(positions, dim):
    inv_freq = ROPE_THETA ** (-jnp.arange(0, dim, 2, dtype=jnp.float32) / dim)
    ang = positions.astype(jnp.float32)[..., None] * inv_freq
    return jnp.cos(ang)[:, :, None, :], jnp.sin(ang)[:, :, None, :]


def apply_rope(x, cos, sin):
    xf = x.astype(jnp.float32)
    x1, x2 = jnp.split(xf, 2, axis=-1)
    return jnp.concatenate([x1 * cos - x2 * sin, x2 * cos + x1 * sin], axis=-1).astype(x.dtype)


def conformer_conv(u, dw_w, dw_b, ln_g, ln_b):
    a, gate = jnp.split(u, 2, axis=-1)
    h = a * jax.nn.sigmoid(gate)
    h = causal_dwconv(h, dw_w, dw_b)
    h = layer_norm(h, ln_g, ln_b)
    return jax.nn.silu(h)


def diff_attention(q, k, v, cos, sin, qn_g, kn_g, lam, sub_g, lam_init):
    bsz, s, _ = q.shape
    h, d = DIFF_HEADS, DIFF_HD
    q = apply_rope(rms_norm(q.reshape(bsz, s, h * 2, d), qn_g), cos, sin) * (d ** -0.5)
    k = apply_rope(rms_norm(k.reshape(bsz, s, h * 2, d), kn_g), cos, sin)
    q = q.reshape(bsz, s, h, 2, d).transpose(0, 2, 3, 1, 4)
    k = k.reshape(bsz, s, h, 2, d).transpose(0, 2, 3, 1, 4)
    v = v.reshape(bsz, s, h, DIFF_VD).transpose(0, 2, 1, 3)
    nb = s // Q_BLOCK
    qb = q.reshape(bsz, h, 2, nb, Q_BLOCK, d).transpose(3, 0, 1, 2, 4, 5)
    kpos = jnp.arange(s)

    def block(args):
        qi, bi = args
        sc = jnp.einsum("bhmqd,bhmkd->bhmqk", qi, k, preferred_element_type=jnp.float32)
        qpos = bi * Q_BLOCK + jnp.arange(Q_BLOCK)
        sc = jnp.where(kpos[None, :] <= qpos[:, None], sc, -jnp.inf)
        pr = jax.nn.softmax(sc, axis=-1)
        a = pr[:, :, 0] - lam * pr[:, :, 1]
        return jnp.einsum("bhqk,bhkv->bhqv", a.astype(v.dtype), v)

    o = lax.map(block, (qb, jnp.arange(nb)))
    o = o.transpose(1, 0, 3, 2, 4).reshape(bsz, s, h, DIFF_VD)
    o = rms_norm(o, sub_g) * (1.0 - lam_init)
    return o.reshape(bsz, s, DIFF_W)


def ssd_mixer(z, xbc, dt_raw, conv_w, conv_b, dt_bias, a_log, d_skip, norm_g):
    bsz, s, _ = z.shape
    g, n, p_ = SSD_GROUPS, SSD_STATE, SSD_HD
    r = SSD_HEADS // SSD_GROUPS
    lc = SSD_CHUNK
    nc = s // lc
    f32 = jnp.float32
    xbc = jax.nn.silu(causal_dwconv(xbc, conv_w, conv_b))
    xs, bm, cm = jnp.split(xbc, [SSD_W, SSD_W + g * n], axis=-1)
    dt = jax.nn.softplus(dt_raw.astype(f32) + dt_bias.astype(f32))
    a = -jnp.exp(a_log.astype(f32))
    x6 = xs.reshape(bsz, nc, lc, g, r, p_)
    xdt = x6.astype(f32) * dt.reshape(bsz, nc, lc, g, r)[..., None]
    bc = bm.reshape(bsz, nc, lc, g, n).astype(f32)
    cc = cm.reshape(bsz, nc, lc, g, n).astype(f32)
    da = (dt * a).reshape(bsz, nc, lc, g, r).transpose(0, 3, 4, 1, 2)
    cs = jnp.cumsum(da, axis=-1)
    causal = jnp.tril(jnp.ones((lc, lc), dtype=bool))
    seg = cs[..., :, None] - cs[..., None, :]
    lmat = jnp.exp(jnp.where(causal, seg, -jnp.inf))
    cb = jnp.einsum("bclgn,bcsgn->bgcls", cc, bc)
    y_diag = jnp.einsum("bgcls,bgrcls,bcsgrp->bclgrp", cb, lmat, xdt)
    decay = jnp.exp(cs[..., -1:] - cs)
    states = jnp.einsum("bcsgn,bgrcs,bcsgrp->cbgrpn", bc, decay, xdt)
    chunk_decay = jnp.exp(cs[..., -1]).transpose(3, 0, 1, 2)

    def step(hstate, inp):
        st, dec = inp
        return hstate * dec[..., None, None] + st, hstate

    h0 = jnp.zeros((bsz, g, r, p_, n), dtype=states.dtype)
    _, prev = lax.scan(step, h0, (states, chunk_decay))
    y_off = jnp.einsum("bclgn,cbgrpn,bgrcl->bclgrp", cc, prev, jnp.exp(cs))
    y = (y_diag + y_off).reshape(bsz, s, SSD_HEADS, p_)
    y = y + xs.reshape(bsz, s, SSD_HEADS, p_).astype(f32) * d_skip.astype(f32)[:, None]
    y = y.reshape(bsz, s, SSD_W) * jax.nn.silu(z.astype(f32))
    yg = y.reshape(bsz, s, g, SSD_W // g)
    yg = yg * lax.rsqrt(jnp.mean(yg * yg, axis=-1, keepdims=True) + EPS)
    y = yg.reshape(bsz, s, SSD_W) * norm_g.astype(f32)
    return y.astype(z.dtype)


def conv_glu_ffn(h, w_up, dw_w, dw_b, w_down):
    u = h @ w_up
    gate, val = jnp.split(u, 2, axis=-1)
    gate = causal_dwconv(gate, dw_w, dw_b)
    return (jax.nn.silu(gate) * val) @ w_down


def setup_inputs(seed: int = 0) -> dict:
    key = jax.random.key(seed)
    ks = jax.random.split(key, 40)
    f32 = jnp.float32

    def nrm(k, shape, scale):
        return jax.random.normal(k, shape, f32) * scale

    def gain(k, shape):
        return 1.0 + 0.02 * jax.random.normal(k, shape, f32)

    x = nrm(ks[0], (BATCH, SEQ, D_MODEL), 1.0)
    p = nrm(ks[1], (DEPTH, BATCH, SEQ, D_PLE), 1.0)
    positions = jnp.broadcast_to(jnp.arange(SEQ, dtype=jnp.int32), (BATCH, SEQ))
    dt0 = jnp.exp(jax.random.uniform(ks[17], (DEPTH, SSD_HEADS), f32)
                  * (math.log(0.1) - math.log(0.001)) + math.log(0.001))
    return {
        "x": x,
        "p": p,
        "positions": positions,
        "g_mix": gain(ks[2], (DEPTH, D_MODEL)),
        "w_in": nrm(ks[3], (DEPTH, D_MODEL, D_IN), D_MODEL ** -0.5),
        "conv_dw_w": nrm(ks[4], (DEPTH, CONV_WIDTH, CONV_CH), CONV_WIDTH ** -0.5),
        "conv_dw_b": nrm(ks[5], (DEPTH, CONV_CH), 0.01),
        "conv_ln_g": gain(ks[6], (DEPTH, CONV_CH)),
        "conv_ln_b": nrm(ks[7], (DEPTH, CONV_CH), 0.01),
        "q_norm_g": gain(ks[8], (DEPTH, DIFF_HD)),
        "k_norm_g": gain(ks[9], (DEPTH, DIFF_HD)),
        "lam_q1": nrm(ks[10], (DEPTH, DIFF_HD), 0.1),
        "lam_k1": nrm(ks[11], (DEPTH, DIFF_HD), 0.1),
        "lam_q2": nrm(ks[12], (DEPTH, DIFF_HD), 0.1),
        "lam_k2": nrm(ks[13], (DEPTH, DIFF_HD), 0.1),
        "attn_subln_g": gain(ks[14], (DEPTH, DIFF_VD)),
        "ssd_conv_w": nrm(ks[15], (DEPTH, SSD_CONV, SSD_XBC), SSD_CONV ** -0.5),
        "ssd_conv_b": nrm(ks[16], (DEPTH, SSD_XBC), 0.01),
        "ssd_dt_bias": dt0 + jnp.log(-jnp.expm1(-dt0)),
        "ssd_a_log": jnp.log(jax.random.uniform(ks[18], (DEPTH, SSD_HEADS), f32, 1.0, 16.0)),
        "ssd_d": 1.0 + 0.1 * jax.random.normal(ks[19], (DEPTH, SSD_HEADS), f32),
        "ssd_norm_g": gain(ks[20], (DEPTH, SSD_W)),
        "w_out": nrm(ks[21], (DEPTH, D_MIX, D_MODEL), 0.5 * D_MIX ** -0.5),
        "g_ffn": gain(ks[22], (DEPTH, D_MODEL)),
        "w_up": nrm(ks[23], (DEPTH, D_MODEL, 2 * D_FF), D_MODEL ** -0.5),
        "ffn_dw_w": nrm(ks[24], (DEPTH, FFN_CONV, D_FF), FFN_CONV ** -0.5),
        "ffn_dw_b": nrm(ks[25], (DEPTH, D_FF), 0.01),
        "w_down": nrm(ks[26], (DEPTH, D_FF, D_MODEL), 0.5 * D_FF ** -0.5),
        "g_ple": gain(ks[27], (DEPTH, D_MODEL)),
        "w_ple_gate": nrm(ks[28], (DEPTH, D_MODEL, D_MODEL), D_MODEL ** -0.5),
        "w_ple": nrm(ks[29], (DEPTH, D_PLE, D_MODEL), 0.5 * D_PLE ** -0.5),
    }


def reference(x, p, positions, g_mix, w_in, conv_dw_w, conv_dw_b, conv_ln_g, conv_ln_b,
              q_norm_g, k_norm_g, lam_q1, lam_k1, lam_q2, lam_k2, attn_subln_g,
              ssd_conv_w, ssd_conv_b, ssd_dt_bias, ssd_a_log, ssd_d, ssd_norm_g,
              w_out, g_ffn, w_up, ffn_dw_w, ffn_dw_b, w_down, g_ple, w_ple_gate, w_ple):
    cos, sin = rope_tables(positions, DIFF_HD)
    split_idx = []
    acc = 0
    for sz in SPLIT_SIZES[:-1]:
        acc += sz
        split_idx.append(acc)
    for i in range(DEPTH):
        lam_init = 0.8 - 0.6 * math.exp(-0.3 * i)
        hn = rms_norm(x, g_mix[i])
        u = hn @ w_in[i]
        u_conv, q, k, v, z, xbc, dt_raw = jnp.split(u, split_idx, axis=-1)
        y_conv = conformer_conv(u_conv, conv_dw_w[i], conv_dw_b[i], conv_ln_g[i], conv_ln_b[i])
        lam = (jnp.exp(jnp.sum(lam_q1[i].astype(jnp.float32) * lam_k1[i].astype(jnp.float32)))
               - jnp.exp(jnp.sum(lam_q2[i].astype(jnp.float32) * lam_k2[i].astype(jnp.float32)))
               + lam_init)
        y_diff = diff_attention(q, k, v, cos, sin, q_norm_g[i], k_norm_g[i], lam,
                                attn_subln_g[i], lam_init)
        y_ssd = ssd_mixer(z, xbc, dt_raw, ssd_conv_w[i], ssd_conv_b[i], ssd_dt_bias[i],
                          ssd_a_log[i], ssd_d[i], ssd_norm_g[i])
        x = x + jnp.concatenate([y_conv, y_diff, y_ssd], axis=-1) @ w_out[i]
        x = x + conv_glu_ffn(rms_norm(x, g_ffn[i]), w_up[i], ffn_dw_w[i], ffn_dw_b[i], w_down[i])
        x = x + (p[i] @ w_ple[i]) * jax.nn.sigmoid(rms_norm(x, g_ple[i]) @ w_ple_gate[i])
    return x
```

```python
import functools
import math

import jax
import jax.numpy as jnp
from jax import lax
from jax.experimental import pallas as pl
from jax.experimental.pallas import tpu as pltpu

F32 = jnp.float32
BF16 = jnp.bfloat16

CONV_CH = 256
CONV_WIDTH = 31
DIFF_HEADS = 4
DIFF_HD = 32
DIFF_VD = 64
DIFF_W = 256
SSD_HEADS = 8
SSD_HD = 64
SSD_W = 512
SSD_GROUPS = 2
SSD_STATE = 128
SSD_CONV = 4
SSD_CHUNK = 128
SSD_XBC = 1024
FFN_CONV = 3
ROPE_THETA = 10000.0
EPS = 1e-6

LANES = 128
SUBLANES = 8
VMEM_LIMIT_BYTES = 56 * 1024 * 1024

CONV_HALO = 32
SMALL_HALO = 8
ROW_CHUNK = SSD_CHUNK


def _dot(a, b):
    return jnp.dot(a, b, preferred_element_type=F32)


def _dot_nt(a, b):
    return lax.dot_general(a, b, (((1,), (1,)), ((), ())), preferred_element_type=F32)


def _split_bf16(x, terms):
    parts = []
    r = x
    for t in range(terms):
        p = r.astype(BF16)
        parts.append(p)
        if t + 1 < terms:
            r = r - p.astype(F32)
    return parts


def _dot_exact_lhs(x, m, terms):
    parts = _split_bf16(x, terms)
    acc = _dot(parts[0], m)
    for p in parts[1:]:
        acc = acc + _dot(p, m)
    return acc


def _dot_exact_rhs(m, x, terms):
    parts = _split_bf16(x, terms)
    acc = _dot(m, parts[0])
    for p in parts[1:]:
        acc = acc + _dot(m, p)
    return acc


def _sigmoid(x):
    return 1.0 / (1.0 + jnp.exp(-x))


def _silu(x):
    return x * _sigmoid(x)


def _softplus(x):
    return jnp.maximum(x, 0.0) + jnp.log1p(jnp.exp(-jnp.abs(x)))


def _shift_rows(w, r):
    if r == 0:
        return w
    return pltpu.roll(w, w.shape[0] - r, axis=0)


def _causal_dwconv_rows(window, halo, width, w_ref, bias, rows):
    base = halo - (width - 1)
    acc = None
    for r in range(SUBLANES):
        taps = [o for o in range(base, base + width) if o % SUBLANES == r]
        if not taps:
            continue
        shifted = _shift_rows(window, r)
        for o in taps:
            k = o - base
            a = o - r
            term = w_ref[k:k + 1, :] * shifted[a:a + rows, :]
            acc = term if acc is None else acc + term
    return acc + bias


def _rope_kernel(pos_ref, invf_ref, sgn_ref, cos_ref, sin_ref):
    ang = pos_ref[0].astype(F32) * invf_ref[...]
    cos_ref[0] = jnp.cos(ang)
    sin_ref[0] = jnp.sin(ang) * sgn_ref[...]


def _rope_tables(positions):
    b, s = positions.shape
    t = min(s, 512)
    assert s % t == 0
    half = DIFF_HD // 2
    inv_freq = ROPE_THETA ** (-jnp.arange(0, DIFF_HD, 2, dtype=F32) / DIFF_HD)
    lane = jnp.arange(LANES)
    invf = inv_freq[(lane % DIFF_HD) % half][None, :]
    sgn = jnp.where((lane % DIFF_HD) < half, -1.0, 1.0).astype(F32)[None, :]
    return pl.pallas_call(
        _rope_kernel,
        grid=(b, s // t),
        in_specs=[
            pl.BlockSpec((1, t, 1), lambda i, j: (i, j, 0)),
            pl.BlockSpec((1, LANES), lambda i, j: (0, 0)),
            pl.BlockSpec((1, LANES), lambda i, j: (0, 0)),
        ],
        out_specs=[
            pl.BlockSpec((1, t, LANES), lambda i, j: (i, j, 0)),
            pl.BlockSpec((1, t, LANES), lambda i, j: (i, j, 0)),
        ],
        out_shape=[jax.ShapeDtypeStruct((b, s, LANES), F32)] * 2,
        compiler_params=pltpu.CompilerParams(dimension_semantics=("parallel", "parallel")),
        name="rope_tables",
    )(positions.reshape(b, s, 1), invf, sgn)


def _mix_in_kernel(
        x_ref, cos_ref, sin_ref, gmix_ref,
        wconv_ref, wq_ref, wk_ref, wv_ref, wz_ref, wxbc_ref, wdt_ref, wdtt_ref,
        cw_ref, cb_ref, lng_ref, lnb_ref,
        gq_ref, gk_ref, g32_ref,
        sw_ref, sb_ref, dtb_ref, dtbt_ref, alog_ref, alogt_ref, dskip_ref, ng_ref,
        tril_ref, triu_ref, exp_ref,
        ycs_ref, q_ref, k_ref, v_ref,
        hn_scr, uc_scr, q_scr, k_scr, z_scr, xbc_scr, dt_scr, dtt_scr, hbuf, xbuf, h_scr,
        *, tile):
    n_chunks = tile // ROW_CHUNK
    d_model = x_ref.shape[-1]

    @pl.when(pl.program_id(1) == 0)
    def _():
        hbuf[0:CONV_HALO, :] = jnp.zeros((CONV_HALO, CONV_CH), F32)
        xbuf[0:SMALL_HALO, :] = jnp.zeros((SMALL_HALO, SSD_XBC), F32)
        h_scr[...] = jnp.zeros(h_scr.shape, F32)

    def norm_rows(c, carry):
        rows = pl.ds(pl.multiple_of(c * ROW_CHUNK, ROW_CHUNK), ROW_CHUNK)
        xr = x_ref[0, rows, :]
        ms = jnp.sum(xr * xr, axis=-1, keepdims=True) * (1.0 / d_model)
        hn_scr[rows, :] = (xr * lax.rsqrt(ms + EPS) * gmix_ref[...]).astype(BF16)
        return carry

    lax.fori_loop(0, n_chunks, norm_rows, 0)

    hn = hn_scr[...]
    uc_scr[...] = _dot(hn, wconv_ref[...])
    q_scr[...] = _dot(hn, wq_ref[...])
    k_scr[...] = _dot(hn, wk_ref[...])
    v_ref[0] = _dot(hn, wv_ref[...]).astype(BF16)
    z_scr[...] = _dot(hn, wz_ref[...])
    xbc_scr[...] = _dot(hn, wxbc_ref[...])
    dt_scr[...] = _dot(hn, wdt_ref[...])
    dtt = _dot_nt(wdtt_ref[...], hn)
    for c in range(n_chunks):
        dtt_scr[c] = dtt[:, c * ROW_CHUNK:(c + 1) * ROW_CHUNK]

    lane128 = lax.broadcasted_iota(jnp.int32, (1, LANES), 1)
    lane256 = lax.broadcasted_iota(jnp.int32, (1, 2 * LANES), 1)
    first_half = (lane256 % DIFF_HD) < (DIFF_HD // 2)
    row_i = lax.broadcasted_iota(jnp.int32, (ROW_CHUNK, ROW_CHUNK), 0)
    col_i = lax.broadcasted_iota(jnp.int32, (ROW_CHUNK, ROW_CHUNK), 1)
    causal = col_i <= row_i
    a_row = -jnp.exp(alog_ref[...])
    a_col = -jnp.exp(alogt_ref[...])

    def norm_rope(t, g_ref, cos, sin):
        ms = _dot_exact_lhs(t * t, g32_ref[...], 2)
        tn = t * lax.rsqrt(ms + EPS) * g_ref[...]
        fwd = pltpu.roll(tn, 2 * LANES - DIFF_HD // 2, axis=1)
        bwd = pltpu.roll(tn, DIFF_HD // 2, axis=1)
        rot = jnp.where(first_half, fwd, bwd)
        lo = tn[:, :LANES] * cos + rot[:, :LANES] * sin
        hi = tn[:, LANES:] * cos + rot[:, LANES:] * sin
        return jnp.concatenate([lo, hi], axis=1).astype(BF16)

    def mix_rows(c, carry):
        r0 = pl.multiple_of(c * ROW_CHUNK, ROW_CHUNK)
        rows = pl.ds(r0, ROW_CHUNK)

        uc = uc_scr[rows, :]
        hbuf[pl.ds(r0 + CONV_HALO, ROW_CHUNK), :] = uc[:, :CONV_CH] * _sigmoid(uc[:, CONV_CH:])
        win = hbuf[pl.ds(r0, ROW_CHUNK + CONV_HALO), :]
        hc = _causal_dwconv_rows(win, CONV_HALO, CONV_WIDTH, cw_ref, cb_ref[...], ROW_CHUNK)
        mu = jnp.mean(hc, axis=-1, keepdims=True)
        xc = hc - mu
        var = jnp.mean(xc * xc, axis=-1, keepdims=True)
        yln = xc * lax.rsqrt(var + EPS) * lng_ref[...] + lnb_ref[...]
        ycs_ref[0, rows, 0:CONV_CH] = _silu(yln).astype(BF16)

        cos = cos_ref[0, rows, :]
        sin = sin_ref[0, rows, :]
        q_ref[0, rows, :] = norm_rope(q_scr[rows, :], gq_ref, cos, sin)
        k_ref[0, rows, :] = norm_rope(k_scr[rows, :], gk_ref, cos, sin)

        xbuf[pl.ds(r0 + SMALL_HALO, ROW_CHUNK), :] = xbc_scr[rows, :]
        xwin = xbuf[pl.ds(r0, ROW_CHUNK + SMALL_HALO), :]
        xbc = _silu(_causal_dwconv_rows(xwin, SMALL_HALO, SSD_CONV, sw_ref, sb_ref[...], ROW_CHUNK))
        xs = xbc[:, :SSD_W]
        bmat = xbc[:, SSD_W:SSD_W + SSD_GROUPS * SSD_STATE]
        cmat = xbc[:, SSD_W + SSD_GROUPS * SSD_STATE:]

        dt = _softplus(dt_scr[rows, :] + dtb_ref[...])
        cs = _dot_exact_rhs(tril_ref[...], dt * a_row, 3)
        dt_t = _softplus(dtt_scr[c] + dtbt_ref[...])
        cs_t = _dot_exact_lhs(dt_t * a_col, triu_ref[...], 3)

        dt_w = _dot_exact_lhs(dt, exp_ref[...], 2)
        cs_w = _dot_exact_lhs(cs, exp_ref[...], 3)
        cs_last = cs_w[ROW_CHUNK - 1:ROW_CHUNK, :]
        xdt = xs * dt_w
        xdt_b = xdt.astype(BF16)
        xdec_b = (xdt * jnp.exp(cs_last - cs_w)).astype(BF16)
        ecs = jnp.exp(cs_w)
        chunk_decay = jnp.exp(cs_last)
        z = z_scr[rows, :]

        gw = SSD_W // SSD_GROUPS
        for g in range(SSD_GROUPS):
            bg = bmat[:, g * SSD_STATE:(g + 1) * SSD_STATE]
            cg_b = cmat[:, g * SSD_STATE:(g + 1) * SSD_STATE].astype(BF16)
            cb = _dot_nt(cg_b, bg.astype(BF16))
            yd_parts = []
            for pr in range(gw // LANES):
                xpair = xdt_b[:, g * gw + pr * LANES:g * gw + (pr + 1) * LANES]
                outs = []
                for hh in range(LANES // SSD_HD):
                    h = (g * gw + pr * LANES) // SSD_HD + hh
                    seg = cs[:, h:h + 1] - cs_t[h:h + 1, :]
                    lmat = jnp.exp(jnp.where(causal, seg, -jnp.inf))
                    outs.append(_dot((cb * lmat).astype(BF16), xpair))
                yd_parts.append(jnp.where(lane128 < SSD_HD, outs[0], outs[1]))
            yd = jnp.concatenate(yd_parts, axis=1)
            hstate = h_scr[g]
            yo = _dot(cg_b, hstate.astype(BF16)) * ecs[:, g * gw:(g + 1) * gw]
            st = _dot(bg.T.astype(BF16), xdec_b[:, g * gw:(g + 1) * gw])
            h_scr[g] = hstate * chunk_decay[:, g * gw:(g + 1) * gw] + st
            yg = (yd + yo + xs[:, g * gw:(g + 1) * gw] * dskip_ref[:, g * gw:(g + 1) * gw])
            yg = yg * _silu(z[:, g * gw:(g + 1) * gw])
            ms = jnp.mean(yg * yg, axis=-1, keepdims=True)
            yn = yg * lax.rsqrt(ms + EPS) * ng_ref[:, g * gw:(g + 1) * gw]
            ycs_ref[0, rows, CONV_CH + g * gw:CONV_CH + (g + 1) * gw] = yn.astype(BF16)
        return carry

    lax.fori_loop(0, n_chunks, mix_rows, 0)

    hbuf[0:CONV_HALO, :] = hbuf[tile:tile + CONV_HALO, :]
    xbuf[0:SMALL_HALO, :] = xbuf[tile:tile + SMALL_HALO, :]


def _const_spec(shape):
    nd = len(shape)
    return pl.BlockSpec(shape, lambda *_: (0,) * nd, pipeline_mode=pl.Buffered(1))


def _mix_in(x, cos_t, sin_t, lp, consts, tile):
    b, s, d = x.shape
    args = [
        lp["g_mix"],
        lp["w_conv"], lp["w_q"], lp["w_k"], lp["w_v"], lp["w_z"], lp["w_xbc"], lp["w_dt"], lp["w_dtt"],
        lp["conv_w"], lp["conv_b"], lp["conv_ln_g"], lp["conv_ln_b"],
        lp["gq"], lp["gk"], consts["g32"],
        lp["ssd_w"], lp["ssd_b"], lp["dt_bias"], lp["dt_bias_t"], lp["a_log"], lp["a_log_t"],
        lp["d_skip"], lp["ssd_norm_g"],
        consts["tril"], consts["triu"], consts["expand"],
    ]
    seq_map = lambda i, j: (i, j, 0)
    in_specs = [
        pl.BlockSpec((1, tile, d), seq_map),
        pl.BlockSpec((1, tile, LANES), seq_map),
        pl.BlockSpec((1, tile, LANES), seq_map),
    ] + [_const_spec(a.shape) for a in args]
    out_shape = [
        jax.ShapeDtypeStruct((b, s, CONV_CH + SSD_W), BF16),
        jax.ShapeDtypeStruct((b, s, DIFF_W), BF16),
        jax.ShapeDtypeStruct((b, s, DIFF_W), BF16),
        jax.ShapeDtypeStruct((b, s, DIFF_W), BF16),
    ]
    out_specs = [
        pl.BlockSpec((1, tile, CONV_CH + SSD_W), seq_map),
        pl.BlockSpec((1, tile, DIFF_W), seq_map),
        pl.BlockSpec((1, tile, DIFF_W), seq_map),
        pl.BlockSpec((1, tile, DIFF_W), seq_map),
    ]
    scratch = [
        pltpu.VMEM((tile, d), BF16),
        pltpu.VMEM((tile, 2 * CONV_CH), F32),
        pltpu.VMEM((tile, DIFF_W), F32),
        pltpu.VMEM((tile, DIFF_W), F32),
        pltpu.VMEM((tile, SSD_W), F32),
        pltpu.VMEM((tile, SSD_XBC), F32),
        pltpu.VMEM((tile, LANES), F32),
        pltpu.VMEM((tile // ROW_CHUNK, 2 * SUBLANES, ROW_CHUNK), F32),
        pltpu.VMEM((tile + CONV_HALO, CONV_CH), F32),
        pltpu.VMEM((tile + SMALL_HALO, SSD_XBC), F32),
        pltpu.VMEM((SSD_GROUPS, SSD_STATE, SSD_W // SSD_GROUPS), F32),
    ]
    return pl.pallas_call(
        functools.partial(_mix_in_kernel, tile=tile),
        grid=(b, s // tile),
        in_specs=in_specs,
        out_specs=out_specs,
        out_shape=out_shape,
        scratch_shapes=scratch,
        compiler_params=pltpu.CompilerParams(
            dimension_semantics=("parallel", "arbitrary"), vmem_limit_bytes=VMEM_LIMIT_BYTES),
        name="mix_in",
    )(x, cos_t, sin_t, *args)


def _attn_kernel(lamv_ref, subg_ref, q_ref, k_ref, v_ref, o_ref, vext, acc_scr, m_scr, *, blk, lam_init):
    i = pl.program_id(2)
    n_maps = 2 * (LANES // DIFF_VD)

    @pl.when(i == 0)
    def _():
        vext[:, 0:LANES] = v_ref[0]
        vext[:, LANES:2 * LANES] = jnp.ones((vext.shape[0], LANES), BF16)

    lane = lax.broadcasted_iota(jnp.int32, (1, LANES), 1)
    qb = q_ref[0]
    qm = [jnp.where((lane >= j * DIFF_HD) & (lane < (j + 1) * DIFF_HD), qb, jnp.zeros_like(qb))
          for j in range(n_maps)]
    for j in range(n_maps):
        m_scr[j] = jnp.full((blk, LANES), -jnp.inf, F32)
        acc_scr[j] = jnp.zeros((blk, 2 * LANES), F32)

    row_i = lax.broadcasted_iota(jnp.int32, (blk, blk), 0)
    col_i = lax.broadcasted_iota(jnp.int32, (blk, blk), 1)
    causal = col_i <= row_i

    def block(kb, masked):
        k0 = pl.multiple_of(kb * blk, blk)
        kblk = k_ref[0, pl.ds(k0, blk), :]
        vblk = vext[pl.ds(k0, blk), :]
        for j in range(n_maps):
            s = _dot_nt(qm[j], kblk)
            if masked:
                s = jnp.where(causal, s, -jnp.inf)
            cols = [s[:, c * LANES:(c + 1) * LANES] for c in range(blk // LANES)]
            rm = cols[0]
            for cc in cols[1:]:
                rm = jnp.maximum(rm, cc)
            m_old = m_scr[j]
            m_new = jnp.maximum(m_old, jnp.max(rm, axis=-1, keepdims=True))
            alpha = jnp.exp(m_old - m_new)
            p = jnp.concatenate([jnp.exp(cc - m_new) for cc in cols], axis=1).astype(BF16)
            pv = _dot(p, vblk)
            acc_scr[j] = acc_scr[j] * jnp.concatenate([alpha, alpha], axis=1) + pv
            m_scr[j] = m_new

    def body(kb, carry):
        block(kb, False)
        return carry

    lax.fori_loop(0, i, body, 0)
    block(i, True)

    lv = lamv_ref[...]
    lam = (jnp.exp(jnp.sum(lv[0:1] * lv[1:2], axis=-1, keepdims=True))
           - jnp.exp(jnp.sum(lv[2:3] * lv[3:4], axis=-1, keepdims=True)) + lam_init)
    outs = []
    for hh in range(LANES // DIFF_VD):
        a1 = acc_scr[2 * hh]
        a2 = acc_scr[2 * hh + 1]
        outs.append(a1[:, :LANES] / a1[:, LANES:] - lam * (a2[:, :LANES] / a2[:, LANES:]))
    lo_half = lane < DIFF_VD
    o = jnp.where(lo_half, outs[0], outs[1])
    o2 = o * o
    ms_lo = jnp.sum(jnp.where(lo_half, o2, 0.0), axis=-1, keepdims=True)
    ms_hi = jnp.sum(jnp.where(lo_half, 0.0, o2), axis=-1, keepdims=True)
    ms = jnp.where(lo_half, ms_lo, ms_hi) * (1.0 / DIFF_VD)
    o_ref[0] = (o * lax.rsqrt(ms + EPS) * subg_ref[...]).astype(BF16)


def _diff_attn(q, k, v, lamv, subg, lam_init, blk):
    b, s, w = q.shape
    n_pairs = w // LANES
    return pl.pallas_call(
        functools.partial(_attn_kernel, blk=blk, lam_init=lam_init),
        grid=(b, n_pairs, s // blk),
        in_specs=[
            pl.BlockSpec((4, LANES), lambda bi, pi, i: (0, 0)),
            pl.BlockSpec((1, LANES), lambda bi, pi, i: (0, 0)),
            pl.BlockSpec((1, blk, LANES), lambda bi, pi, i: (bi, i, pi)),
            pl.BlockSpec((1, s, LANES), lambda bi, pi, i: (bi, 0, pi)),
            pl.BlockSpec((1, s, LANES), lambda bi, pi, i: (bi, 0, pi)),
        ],
        out_specs=pl.BlockSpec((1, blk, LANES), lambda bi, pi, i: (bi, i, pi)),
        out_shape=jax.ShapeDtypeStruct((b, s, w), BF16),
        scratch_shapes=[
            pltpu.VMEM((s, 2 * LANES), BF16),
            pltpu.VMEM((4, blk, 2 * LANES), F32),
            pltpu.VMEM((4, blk, LANES), F32),
        ],
        compiler_params=pltpu.CompilerParams(
            dimension_semantics=("parallel", "parallel", "arbitrary"), vmem_limit_bytes=VMEM_LIMIT_BYTES),
        name="diff_attn",
    )(lamv, subg, q, k, v)


def _mix_out_kernel(
        x_ref, ycs_ref, yd_ref, p_ref,
        wo_cs_ref, wo_d_ref, gffn_ref, wg_ref, wv_ref, fw_ref, fb_ref, wdown_ref,
        gple_ref, wpg_ref, wple_ref,
        o_ref,
        h_scr, gate_buf, val_scr, hid_scr,
        *, tile):
    n_chunks = tile // ROW_CHUNK
    d_model = x_ref.shape[-1]
    d_ff = wdown_ref.shape[0]

    @pl.when(pl.program_id(1) == 0)
    def _():
        gate_buf[0:SMALL_HALO, :] = jnp.zeros((SMALL_HALO, d_ff), F32)

    def rms(v, g_ref):
        ms = jnp.sum(v * v, axis=-1, keepdims=True) * (1.0 / d_model)
        return (v * lax.rsqrt(ms + EPS) * g_ref[...]).astype(BF16)

    o_ref[0] = x_ref[0] + _dot(ycs_ref[0], wo_cs_ref[...]) + _dot(yd_ref[0], wo_d_ref[...])

    def norm_rows(c, carry):
        rows = pl.ds(pl.multiple_of(c * ROW_CHUNK, ROW_CHUNK), ROW_CHUNK)
        h_scr[rows, :] = rms(o_ref[0, rows, :], gffn_ref)
        return carry

    lax.fori_loop(0, n_chunks, norm_rows, 0)

    hval = h_scr[...]
    gate_buf[SMALL_HALO:SMALL_HALO + tile, :] = _dot(hval, wg_ref[...])
    val_scr[...] = _dot(hval, wv_ref[...])

    def ffn_rows(c, carry):
        r0 = pl.multiple_of(c * ROW_CHUNK, ROW_CHUNK)
        win = gate_buf[pl.ds(r0, ROW_CHUNK + SMALL_HALO), :]
        gate = _causal_dwconv_rows(win, SMALL_HALO, FFN_CONV, fw_ref, fb_ref[...], ROW_CHUNK)
        hid_scr[pl.ds(r0, ROW_CHUNK), :] = (_silu(gate) * val_scr[pl.ds(r0, ROW_CHUNK), :]).astype(BF16)
        return carry

    lax.fori_loop(0, n_chunks, ffn_rows, 0)
    gate_buf[0:SMALL_HALO, :] = gate_buf[tile:tile + SMALL_HALO, :]

    o_ref[0] = o_ref[0] + _dot(hid_scr[...], wdown_ref[...])

    def norm_rows2(c, carry):
        rows = pl.ds(pl.multiple_of(c * ROW_CHUNK, ROW_CHUNK), ROW_CHUNK)
        h_scr[rows, :] = rms(o_ref[0, rows, :], gple_ref)
        return carry

    lax.fori_loop(0, n_chunks, norm_rows2, 0)
    gate = _sigmoid(_dot(h_scr[...], wpg_ref[...]))
    o_ref[0] = o_ref[0] + _dot(p_ref[0].astype(BF16), wple_ref[...]) * gate


def _mix_out(x, ycs, ydiff, p, lp, tile):
    b, s, d = x.shape
    d_ff = lp["w_down"].shape[0]
    args = [
        lp["wo_cs"], lp["wo_d"], lp["g_ffn"], lp["w_gate"], lp["w_val"], lp["ffn_w"], lp["ffn_b"], lp["w_down"],
        lp["g_ple"], lp["w_ple_gate"], lp["w_ple"],
    ]
    seq_map = lambda i, j: (i, j, 0)
    in_specs = [
        pl.BlockSpec((1, tile, d), seq_map),
        pl.BlockSpec((1, tile, ycs.shape[-1]), seq_map),
        pl.BlockSpec((1, tile, ydiff.shape[-1]), seq_map),
        pl.BlockSpec((1, tile, p.shape[-1]), seq_map),
    ] + [_const_spec(a.shape) for a in args]
    return pl.pallas_call(
        functools.partial(_mix_out_kernel, tile=tile),
        grid=(b, s // tile),
        in_specs=in_specs,
        out_specs=pl.BlockSpec((1, tile, d), seq_map),
        out_shape=jax.ShapeDtypeStruct((b, s, d), F32),
        scratch_shapes=[
            pltpu.VMEM((tile, d), BF16),
            pltpu.VMEM((tile + SMALL_HALO, d_ff), F32),
            pltpu.VMEM((tile, d_ff), F32),
            pltpu.VMEM((tile, d_ff), BF16),
        ],
        compiler_params=pltpu.CompilerParams(
            dimension_semantics=("parallel", "arbitrary"), vmem_limit_bytes=VMEM_LIMIT_BYTES),
        name="mix_out",
    )(x, ycs, ydiff, p, *args)


def _pad_rows(a, rows):
    return jnp.pad(a, ((0, rows - a.shape[0]), (0, 0)))


def _pad_lanes(a, lanes):
    return jnp.pad(a, ((0, 0), (0, lanes - a.shape[1])))


def _constants():
    r = jnp.arange(ROW_CHUNK)
    tril = (r[None, :] <= r[:, None]).astype(BF16)
    lane = jnp.arange(LANES)
    col = jnp.arange(SSD_W)
    expand = (lane[:, None] == (col[None, :] // SSD_HD)).astype(BF16)
    c2 = jnp.arange(DIFF_W)
    g32 = ((c2[:, None] // DIFF_HD) == (c2[None, :] // DIFF_HD)).astype(F32) / DIFF_HD
    return {"tril": tril, "triu": tril.T, "expand": expand, "g32": g32.astype(BF16)}


def _layer_params(i, w):
    d = w["w_in"].shape[1]
    w_in = w["w_in"][i]
    edges = [0]
    for sz in (2 * CONV_CH, DIFF_W, DIFF_W, DIFF_W, SSD_W, SSD_XBC, SSD_HEADS):
        edges.append(edges[-1] + sz)
    cols = [w_in[:, edges[j]:edges[j + 1]] for j in range(7)]
    w_dt = cols[6]
    row = lambda v: v[i][None, :].astype(F32)
    n_rep = DIFF_W // DIFF_HD
    w_up = w["w_up"][i]
    d_ff = w_up.shape[1] // 2
    w_out = w["w_out"][i]
    lam_init = 0.8 - 0.6 * math.exp(-0.3 * i)
    return {
        "g_mix": row(w["g_mix"]),
        "w_conv": cols[0].astype(BF16), "w_q": cols[1].astype(BF16), "w_k": cols[2].astype(BF16),
        "w_v": cols[3].astype(BF16), "w_z": cols[4].astype(BF16), "w_xbc": cols[5].astype(BF16),
        "w_dt": _pad_lanes(w_dt, LANES).astype(BF16),
        "w_dtt": _pad_rows(w_dt.T, 2 * SUBLANES).astype(BF16),
        "conv_w": _pad_rows(w["conv_dw_w"][i], 32).astype(F32),
        "conv_b": row(w["conv_dw_b"]), "conv_ln_g": row(w["conv_ln_g"]), "conv_ln_b": row(w["conv_ln_b"]),
        "gq": jnp.tile(w["q_norm_g"][i].astype(F32), n_rep)[None, :] * (DIFF_HD ** -0.5),
        "gk": jnp.tile(w["k_norm_g"][i].astype(F32), n_rep)[None, :],
        "ssd_w": _pad_rows(w["ssd_conv_w"][i], SUBLANES).astype(F32),
        "ssd_b": row(w["ssd_conv_b"]),
        "dt_bias": _pad_lanes(row(w["ssd_dt_bias"]), LANES),
        "dt_bias_t": jnp.broadcast_to(_pad_rows(w["ssd_dt_bias"][i][:, None].astype(F32), 2 * SUBLANES),
                                      (2 * SUBLANES, ROW_CHUNK)),
        "a_log": _pad_lanes(row(w["ssd_a_log"]), LANES),
        "a_log_t": jnp.broadcast_to(_pad_rows(w["ssd_a_log"][i][:, None].astype(F32), 2 * SUBLANES),
                                    (2 * SUBLANES, ROW_CHUNK)),
        "d_skip": jnp.repeat(w["ssd_d"][i].astype(F32), SSD_HD)[None, :],
        "ssd_norm_g": row(w["ssd_norm_g"]),
        "lamv": _pad_lanes(jnp.stack([w["lam_q1"][i], w["lam_k1"][i], w["lam_q2"][i], w["lam_k2"][i]]).astype(F32),
                           LANES),
        "subg": jnp.tile(w["attn_subln_g"][i].astype(F32), LANES // DIFF_VD)[None, :] * (1.0 - lam_init),
        "lam_init": lam_init,
        "wo_cs": jnp.concatenate([w_out[:CONV_CH], w_out[CONV_CH + DIFF_W:]], axis=0).astype(BF16),
        "wo_d": w_out[CONV_CH:CONV_CH + DIFF_W].astype(BF16),
        "g_ffn": row(w["g_ffn"]),
        "w_gate": w_up[:, :d_ff].astype(BF16), "w_val": w_up[:, d_ff:].astype(BF16),
        "ffn_w": _pad_rows(w["ffn_dw_w"][i], SUBLANES).astype(F32), "ffn_b": row(w["ffn_dw_b"]),
        "w_down": w["w_down"][i].astype(BF16),
        "g_ple": row(w["g_ple"]), "w_ple_gate": w["w_ple_gate"][i].astype(BF16), "w_ple": w["w_ple"][i].astype(BF16),
    }


def kernel(x, p, positions, g_mix, w_in, conv_dw_w, conv_dw_b, conv_ln_g, conv_ln_b, q_norm_g, k_norm_g, lam_q1, lam_k1, lam_q2, lam_k2, attn_subln_g, ssd_conv_w, ssd_conv_b, ssd_dt_bias, ssd_a_log, ssd_d, ssd_norm_g, w_out, g_ffn, w_up, ffn_dw_w, ffn_dw_b, w_down, g_ple, w_ple_gate, w_ple):
    w = dict(g_mix=g_mix, w_in=w_in, conv_dw_w=conv_dw_w, conv_dw_b=conv_dw_b, conv_ln_g=conv_ln_g,
             conv_ln_b=conv_ln_b, q_norm_g=q_norm_g, k_norm_g=k_norm_g, lam_q1=lam_q1, lam_k1=lam_k1,
             lam_q2=lam_q2, lam_k2=lam_k2, attn_subln_g=attn_subln_g, ssd_conv_w=ssd_conv_w,
             ssd_conv_b=ssd_conv_b, ssd_dt_bias=ssd_dt_bias, ssd_a_log=ssd_a_log, ssd_d=ssd_d,
             ssd_norm_g=ssd_norm_g, w_out=w_out, g_ffn=g_ffn, w_up=w_up, ffn_dw_w=ffn_dw_w,
             ffn_dw_b=ffn_dw_b, w_down=w_down, g_ple=g_ple, w_ple_gate=w_ple_gate, w_ple=w_ple)
    b, s, d = x.shape
    depth = w_in.shape[0]
    tile = min(s, 512)
    blk = min(s, 256)
    assert s % tile == 0 and tile % ROW_CHUNK == 0 and s % blk == 0 and blk % LANES == 0
    consts = _constants()
    cos_t, sin_t = _rope_tables(positions)
    for i in range(depth):
        lp = _layer_params(i, w)
        ycs, q, k, v = _mix_in(x, cos_t, sin_t, lp, consts, tile)
        ydiff = _diff_attn(q, k, v, lp["lamv"], lp["subg"], lp["lam_init"], blk)
        x = _mix_out(x, ycs, ydiff, p[i], lp, tile)
    return x
```

```python
import functools
import math

import jax
import jax.numpy as jnp
from jax import lax
from jax.experimental import pallas as pl
from jax.experimental.pallas import tpu as pltpu

F32 = jnp.float32
BF16 = jnp.bfloat16

CONV_CH = 256
CONV_WIDTH = 31
DIFF_HEADS = 4
DIFF_HD = 32
DIFF_VD = 64
DIFF_W = 256
SSD_HEADS = 8
SSD_HD = 64
SSD_W = 512
SSD_GROUPS = 2
SSD_STATE = 128
SSD_CONV = 4
SSD_CHUNK = 128
SSD_XBC = 1024
FFN_CONV = 3
ROPE_THETA = 10000.0
EPS = 1e-6
LOG2_E = math.log2(math.e)

LANES = 128
SUBLANES = 8
VMEM_LIMIT_BYTES = 56 * 1024 * 1024

CONV_HALO = 32
SMALL_HALO = 8
ROW_CHUNK = SSD_CHUNK
FFN_COL_BLOCK = 768


def _dot(a, b):
    return jnp.dot(a, b, preferred_element_type=F32)


def _dot_nt(a, b):
    return lax.dot_general(a, b, (((1,), (1,)), ((), ())), preferred_element_type=F32)


def _split_bf16(x, terms):
    parts = []
    r = x
    for t in range(terms):
        p = r.astype(BF16)
        parts.append(p)
        if t + 1 < terms:
            r = r - p.astype(F32)
    return parts


def _dot_exact_lhs(x, m, terms):
    parts = _split_bf16(x, terms)
    acc = _dot(parts[0], m)
    for p in parts[1:]:
        acc = acc + _dot(p, m)
    return acc


def _dot_exact_rhs(m, x, terms):
    parts = _split_bf16(x, terms)
    acc = _dot(m, parts[0])
    for p in parts[1:]:
        acc = acc + _dot(m, p)
    return acc


def _sigmoid(x):
    return 0.5 * jnp.tanh(0.5 * x) + 0.5


def _silu(x):
    h = 0.5 * x
    return h * jnp.tanh(h) + h


def _softplus(x):
    return jnp.maximum(x, 0.0) + jnp.log1p(jnp.exp(-jnp.abs(x)))


def _shift_rows(w, r):
    if r == 0:
        return w
    return pltpu.roll(w, w.shape[0] - r, axis=0)


def _causal_dwconv_rows(window, halo, width, w_ref, bias, rows):
    base = halo - (width - 1)
    acc = None
    for r in range(SUBLANES):
        taps = [o for o in range(base, base + width) if o % SUBLANES == r]
        if not taps:
            continue
        shifted = _shift_rows(window, r)
        for o in taps:
            k = o - base
            a = o - r
            term = w_ref[k:k + 1, :] * shifted[a:a + rows, :]
            acc = term if acc is None else acc + term
    return acc + bias


def _rope_kernel(pos_ref, invf_ref, sgn_ref, cos_ref, sin_ref):
    ang = pos_ref[0].astype(F32) * invf_ref[...]
    cos_ref[0] = jnp.cos(ang)
    sin_ref[0] = jnp.sin(ang) * sgn_ref[...]


def _rope_tables(positions):
    b, s = positions.shape
    t = min(s, 512)
    assert s % t == 0
    half = DIFF_HD // 2
    inv_freq = ROPE_THETA ** (-jnp.arange(0, DIFF_HD, 2, dtype=F32) / DIFF_HD)
    lane = jnp.arange(LANES)
    invf = inv_freq[(lane % DIFF_HD) % half][None, :]
    sgn = jnp.where((lane % DIFF_HD) < half, -1.0, 1.0).astype(F32)[None, :]
    return pl.pallas_call(
        _rope_kernel,
        grid=(b, s // t),
        in_specs=[
            pl.BlockSpec((1, t, 1), lambda i, j: (i, j, 0)),
            pl.BlockSpec((1, LANES), lambda i, j: (0, 0)),
            pl.BlockSpec((1, LANES), lambda i, j: (0, 0)),
        ],
        out_specs=[
            pl.BlockSpec((1, t, LANES), lambda i, j: (i, j, 0)),
            pl.BlockSpec((1, t, LANES), lambda i, j: (i, j, 0)),
        ],
        out_shape=[jax.ShapeDtypeStruct((b, s, LANES), F32)] * 2,
        compiler_params=pltpu.CompilerParams(dimension_semantics=("parallel", "parallel")),
        name="rope_tables",
    )(positions.reshape(b, s, 1), invf, sgn)


def _mix_in_kernel(
        x_ref, cos_ref, sin_ref, gmix_ref,
        wconv_ref, wq_ref, wk_ref, wv_ref, wz_ref, wxbc_ref, wdt_ref, wdtt_ref,
        cw_ref, cb_ref, lng_ref, lnb_ref,
        gq_ref, gk_ref, g32_ref,
        sw_ref, sb_ref, dtb_ref, dtbt_ref, alog_ref, alogt_ref, dskip_ref, ng_ref,
        tril_ref, triu_ref, exp_ref,
        ycs_ref, q_ref, k_ref, v_ref,
        hn_scr, uc_scr, q_scr, k_scr, z_scr, xbc_scr, dt_scr, dtt_scr, hbuf, xbuf, h_scr,
        *, tile):
    n_chunks = tile // ROW_CHUNK
    d_model = x_ref.shape[-1]

    @pl.when(pl.program_id(1) == 0)
    def _():
        hbuf[0:CONV_HALO, :] = jnp.zeros((CONV_HALO, CONV_CH), F32)
        xbuf[0:SMALL_HALO, :] = jnp.zeros((SMALL_HALO, SSD_XBC), F32)
        h_scr[...] = jnp.zeros(h_scr.shape, F32)

    xr = x_ref[0]
    ms = jnp.sum(xr * xr, axis=-1, keepdims=True) * (1.0 / d_model)
    hn_scr[...] = (xr * lax.rsqrt(ms + EPS) * gmix_ref[...]).astype(BF16)

    hn = hn_scr[...]
    uc_scr[...] = _dot(hn, wconv_ref[...])
    q_scr[...] = _dot(hn, wq_ref[...])
    k_scr[...] = _dot(hn, wk_ref[...])
    v_ref[0] = _dot(hn, wv_ref[...]).astype(BF16)
    z_scr[...] = _dot(hn, wz_ref[...])
    xbc_scr[...] = _dot(hn, wxbc_ref[...])
    dt_scr[...] = _dot(hn, wdt_ref[...])
    dtt = _dot_nt(wdtt_ref[...], hn)
    for c in range(n_chunks):
        dtt_scr[c] = dtt[:, c * ROW_CHUNK:(c + 1) * ROW_CHUNK]

    lane128 = lax.broadcasted_iota(jnp.int32, (1, LANES), 1)
    lane256 = lax.broadcasted_iota(jnp.int32, (1, 2 * LANES), 1)
    first_half = (lane256 % DIFF_HD) < (DIFF_HD // 2)
    row_i = lax.broadcasted_iota(jnp.int32, (ROW_CHUNK, ROW_CHUNK), 0)
    col_i = lax.broadcasted_iota(jnp.int32, (ROW_CHUNK, ROW_CHUNK), 1)
    causal = col_i <= row_i
    a_row = -jnp.exp(alog_ref[...])
    a_col = -jnp.exp(alogt_ref[...])

    def norm_rope(t, g_ref, cos, sin):
        ms = _dot_exact_lhs(t * t, g32_ref[...], 2)
        tn = t * lax.rsqrt(ms + EPS) * g_ref[...]
        fwd = pltpu.roll(tn, 2 * LANES - DIFF_HD // 2, axis=1)
        bwd = pltpu.roll(tn, DIFF_HD // 2, axis=1)
        rot = jnp.where(first_half, fwd, bwd)
        lo = tn[:, :LANES] * cos + rot[:, :LANES] * sin
        hi = tn[:, LANES:] * cos + rot[:, LANES:] * sin
        return jnp.concatenate([lo, hi], axis=1).astype(BF16)

    def mix_rows(c):
        r0 = c * ROW_CHUNK
        rows = pl.ds(r0, ROW_CHUNK)

        uc = uc_scr[rows, :]
        hbuf[pl.ds(r0 + CONV_HALO, ROW_CHUNK), :] = uc[:, :CONV_CH] * _sigmoid(uc[:, CONV_CH:])
        win = hbuf[pl.ds(r0, ROW_CHUNK + CONV_HALO), :]
        hc = _causal_dwconv_rows(win, CONV_HALO, CONV_WIDTH, cw_ref, cb_ref[...], ROW_CHUNK)
        mu = jnp.mean(hc, axis=-1, keepdims=True)
        xc = hc - mu
        var = jnp.mean(xc * xc, axis=-1, keepdims=True)
        yln = xc * lax.rsqrt(var + EPS) * lng_ref[...] + lnb_ref[...]
        ycs_ref[0, rows, 0:CONV_CH] = _silu(yln).astype(BF16)

        cos = cos_ref[0, rows, :]
        sin = sin_ref[0, rows, :]
        q_ref[0, rows, :] = norm_rope(q_scr[rows, :], gq_ref, cos, sin)
        k_ref[0, rows, :] = norm_rope(k_scr[rows, :], gk_ref, cos, sin)

        xbuf[pl.ds(r0 + SMALL_HALO, ROW_CHUNK), :] = xbc_scr[rows, :]
        xwin = xbuf[pl.ds(r0, ROW_CHUNK + SMALL_HALO), :]
        xbc = _silu(_causal_dwconv_rows(xwin, SMALL_HALO, SSD_CONV, sw_ref, sb_ref[...], ROW_CHUNK))
        xs = xbc[:, :SSD_W]
        bmat = xbc[:, SSD_W:SSD_W + SSD_GROUPS * SSD_STATE]
        cmat = xbc[:, SSD_W + SSD_GROUPS * SSD_STATE:]

        dt = _softplus(dt_scr[rows, :] + dtb_ref[...])
        cs = _dot_exact_rhs(tril_ref[...], dt * a_row, 3)
        dt_t = _softplus(dtt_scr[c] + dtbt_ref[...])
        cs_t = _dot_exact_lhs(dt_t * a_col, triu_ref[...], 3)

        dt_w = _dot_exact_lhs(dt, exp_ref[...], 2)
        cs_w = _dot_exact_lhs(cs, exp_ref[...], 3)
        cs_last = cs_w[ROW_CHUNK - 1:ROW_CHUNK, :]
        xdt = xs * dt_w
        xdt_b = xdt.astype(BF16)
        xdec_b = (xdt * jnp.exp(cs_last - cs_w)).astype(BF16)
        ecs = jnp.exp(cs_w)
        chunk_decay = jnp.exp(cs_last)
        z = z_scr[rows, :]

        gw = SSD_W // SSD_GROUPS
        for g in range(SSD_GROUPS):
            bg = bmat[:, g * SSD_STATE:(g + 1) * SSD_STATE]
            cg_b = cmat[:, g * SSD_STATE:(g + 1) * SSD_STATE].astype(BF16)
            cb = _dot_nt(cg_b, bg.astype(BF16))
            yd_parts = []
            for pr in range(gw // LANES):
                xpair = xdt_b[:, g * gw + pr * LANES:g * gw + (pr + 1) * LANES]
                outs = []
                for hh in range(LANES // SSD_HD):
                    h = (g * gw + pr * LANES) // SSD_HD + hh
                    seg = cs[:, h:h + 1] - cs_t[h:h + 1, :]
                    lmat = jnp.exp(jnp.where(causal, seg, -jnp.inf))
                    outs.append(_dot((cb * lmat).astype(BF16), xpair))
                yd_parts.append(jnp.where(lane128 < SSD_HD, outs[0], outs[1]))
            yd = jnp.concatenate(yd_parts, axis=1)
            hstate = h_scr[g]
            yo = _dot(cg_b, hstate.astype(BF16)) * ecs[:, g * gw:(g + 1) * gw]
            st = _dot(bg.T.astype(BF16), xdec_b[:, g * gw:(g + 1) * gw])
            h_scr[g] = hstate * chunk_decay[:, g * gw:(g + 1) * gw] + st
            yg = (yd + yo + xs[:, g * gw:(g + 1) * gw] * dskip_ref[:, g * gw:(g + 1) * gw])
            yg = yg * _silu(z[:, g * gw:(g + 1) * gw])
            ms = jnp.mean(yg * yg, axis=-1, keepdims=True)
            yn = yg * lax.rsqrt(ms + EPS) * ng_ref[:, g * gw:(g + 1) * gw]
            ycs_ref[0, rows, CONV_CH + g * gw:CONV_CH + (g + 1) * gw] = yn.astype(BF16)

    for c in range(n_chunks):
        mix_rows(c)

    hbuf[0:CONV_HALO, :] = hbuf[tile:tile + CONV_HALO, :]
    xbuf[0:SMALL_HALO, :] = xbuf[tile:tile + SMALL_HALO, :]


def _const_spec(shape):
    nd = len(shape)
    return pl.BlockSpec(shape, lambda *_: (0,) * nd, pipeline_mode=pl.Buffered(1))


def _mix_in(x, cos_t, sin_t, lp, consts, tile):
    b, s, d = x.shape
    args = [
        lp["g_mix"],
        lp["w_conv"], lp["w_q"], lp["w_k"], lp["w_v"], lp["w_z"], lp["w_xbc"], lp["w_dt"], lp["w_dtt"],
        lp["conv_w"], lp["conv_b"], lp["conv_ln_g"], lp["conv_ln_b"],
        lp["gq"], lp["gk"], consts["g32"],
        lp["ssd_w"], lp["ssd_b"], lp["dt_bias"], lp["dt_bias_t"], lp["a_log"], lp["a_log_t"],
        lp["d_skip"], lp["ssd_norm_g"],
        consts["tril"], consts["triu"], consts["expand"],
    ]
    seq_map = lambda i, j: (i, j, 0)
    in_specs = [
        pl.BlockSpec((1, tile, d), seq_map),
        pl.BlockSpec((1, tile, LANES), seq_map),
        pl.BlockSpec((1, tile, LANES), seq_map),
    ] + [_const_spec(a.shape) for a in args]
    out_shape = [
        jax.ShapeDtypeStruct((b, s, CONV_CH + SSD_W), BF16),
        jax.ShapeDtypeStruct((b, s, DIFF_W), BF16),
        jax.ShapeDtypeStruct((b, s, DIFF_W), BF16),
        jax.ShapeDtypeStruct((b, s, DIFF_W), BF16),
    ]
    out_specs = [
        pl.BlockSpec((1, tile, CONV_CH + SSD_W), seq_map),
        pl.BlockSpec((1, tile, DIFF_W), seq_map),
        pl.BlockSpec((1, tile, DIFF_W), seq_map),
        pl.BlockSpec((1, tile, DIFF_W), seq_map),
    ]
    scratch = [
        pltpu.VMEM((tile, d), BF16),
        pltpu.VMEM((tile, 2 * CONV_CH), F32),
        pltpu.VMEM((tile, DIFF_W), F32),
        pltpu.VMEM((tile, DIFF_W), F32),
        pltpu.VMEM((tile, SSD_W), F32),
        pltpu.VMEM((tile, SSD_XBC), F32),
        pltpu.VMEM((tile, LANES), F32),
        pltpu.VMEM((tile // ROW_CHUNK, 2 * SUBLANES, ROW_CHUNK), F32),
        pltpu.VMEM((tile + CONV_HALO, CONV_CH), F32),
        pltpu.VMEM((tile + SMALL_HALO, SSD_XBC), F32),
        pltpu.VMEM((SSD_GROUPS, SSD_STATE, SSD_W // SSD_GROUPS), F32),
    ]
    return pl.pallas_call(
        functools.partial(_mix_in_kernel, tile=tile),
        grid=(b, s // tile),
        in_specs=in_specs,
        out_specs=out_specs,
        out_shape=out_shape,
        scratch_shapes=scratch,
        compiler_params=pltpu.CompilerParams(
            dimension_semantics=("parallel", "arbitrary"), vmem_limit_bytes=VMEM_LIMIT_BYTES),
        name="mix_in",
    )(x, cos_t, sin_t, *args)


def _attn_kernel(lamv_ref, subg_ref, q_ref, k_ref, v_ref, o_ref, vext, q4_scr, acc_scr, m_scr, sa, sb,
                 *, blk, lam_init):
    i = pl.program_id(2)
    n_heads = LANES // DIFF_VD
    n_maps = 2 * n_heads
    rows = n_maps * blk
    lane = lax.broadcasted_iota(jnp.int32, (1, LANES), 1)
    lo_half = lane < DIFF_VD

    @pl.when(i == 0)
    def _():
        vb = v_ref[0]
        one = jnp.ones_like(vb)
        vext[0] = jnp.where(lo_half, vb, one)
        vext[1] = jnp.where(lo_half, one, vb)

    qb = q_ref[0]
    for j in range(n_maps):
        keep = (lane >= j * DIFF_HD) & (lane < (j + 1) * DIFF_HD)
        q4_scr[j * blk:(j + 1) * blk, :] = jnp.where(keep, qb, jnp.zeros_like(qb))
    m_scr[...] = jnp.full((rows, LANES), -jnp.inf, F32)
    acc_scr[...] = jnp.zeros((rows, LANES), F32)

    row_i = lax.broadcasted_iota(jnp.int32, (blk, blk), 0)
    col_i = lax.broadcasted_iota(jnp.int32, (blk, blk), 1)
    causal = col_i <= row_i

    def scores(kb, s_ref):
        k0 = pl.multiple_of(kb * blk, blk)
        s_ref[...] = _dot_nt(q4_scr[...], k_ref[0, pl.ds(k0, blk), :])

    def consume(kb, s_ref, masked):
        k0 = pl.multiple_of(kb * blk, blk)
        s = s_ref[...]
        if masked:
            s = jnp.concatenate(
                [jnp.where(causal, s[j * blk:(j + 1) * blk, :], -jnp.inf) for j in range(n_maps)], axis=0)
        cols = [s[:, c * LANES:(c + 1) * LANES] for c in range(blk // LANES)]
        rm = cols[0]
        for cc in cols[1:]:
            rm = jnp.maximum(rm, cc)
        m_old = m_scr[...]
        m_new = jnp.maximum(m_old, jnp.max(rm, axis=-1, keepdims=True))
        alpha = jnp.exp2(m_old - m_new)
        p = jnp.concatenate([jnp.exp2(cc - m_new) for cc in cols], axis=1).astype(BF16)
        hr = rows // n_heads
        pv = jnp.concatenate(
            [_dot(p[h * hr:(h + 1) * hr, :], vext[h, pl.ds(k0, blk), :]) for h in range(n_heads)], axis=0)
        acc_scr[...] = acc_scr[...] * alpha + pv
        m_scr[...] = m_new

    scores(0, sa)

    def pair(t, carry):
        scores(2 * t + 1, sb)
        consume(2 * t, sa, False)
        scores(2 * t + 2, sa)
        consume(2 * t + 1, sb, False)
        return carry

    lax.fori_loop(0, i // 2, pair, 0)

    @pl.when(i % 2 == 0)
    def _():
        consume(i, sa, True)

    @pl.when(i % 2 == 1)
    def _():
        scores(i, sb)
        consume(i - 1, sa, False)
        consume(i, sb, True)

    lv = lamv_ref[...]
    lam = (jnp.exp(jnp.sum(lv[0:1] * lv[1:2], axis=-1, keepdims=True))
           - jnp.exp(jnp.sum(lv[2:3] * lv[3:4], axis=-1, keepdims=True)) + lam_init)
    outs = []
    for h in range(n_heads):
        a1 = acc_scr[(2 * h) * blk:(2 * h + 1) * blk, :]
        a2 = acc_scr[(2 * h + 1) * blk:(2 * h + 2) * blk, :]
        r1 = a1 / pltpu.roll(a1, DIFF_VD, axis=1)
        r2 = a2 / pltpu.roll(a2, DIFF_VD, axis=1)
        outs.append(r1 - lam * r2)
    o = jnp.where(lo_half, outs[0], outs[1])
    o2 = o * o
    ms_lo = jnp.sum(jnp.where(lo_half, o2, 0.0), axis=-1, keepdims=True)
    ms_hi = jnp.sum(jnp.where(lo_half, 0.0, o2), axis=-1, keepdims=True)
    ms = jnp.where(lo_half, ms_lo, ms_hi) * (1.0 / DIFF_VD)
    o_ref[0] = (o * lax.rsqrt(ms + EPS) * subg_ref[...]).astype(BF16)


def _diff_attn(q, k, v, lamv, subg, lam_init, blk):
    b, s, w = q.shape
    n_pairs = w // LANES
    n_heads = LANES // DIFF_VD
    return pl.pallas_call(
        functools.partial(_attn_kernel, blk=blk, lam_init=lam_init),
        grid=(b, n_pairs, s // blk),
        in_specs=[
            pl.BlockSpec((4, LANES), lambda bi, pi, i: (0, 0)),
            pl.BlockSpec((1, LANES), lambda bi, pi, i: (0, 0)),
            pl.BlockSpec((1, blk, LANES), lambda bi, pi, i: (bi, i, pi)),
            pl.BlockSpec((1, s, LANES), lambda bi, pi, i: (bi, 0, pi)),
            pl.BlockSpec((1, s, LANES), lambda bi, pi, i: (bi, 0, pi)),
        ],
        out_specs=pl.BlockSpec((1, blk, LANES), lambda bi, pi, i: (bi, i, pi)),
        out_shape=jax.ShapeDtypeStruct((b, s, w), BF16),
        scratch_shapes=[
            pltpu.VMEM((n_heads, s, LANES), BF16),
            pltpu.VMEM((2 * n_heads * blk, LANES), BF16),
            pltpu.VMEM((2 * n_heads * blk, LANES), F32),
            pltpu.VMEM((2 * n_heads * blk, LANES), F32),
            pltpu.VMEM((2 * n_heads * blk, blk), F32),
            pltpu.VMEM((2 * n_heads * blk, blk), F32),
        ],
        compiler_params=pltpu.CompilerParams(
            dimension_semantics=("parallel", "parallel", "arbitrary"), vmem_limit_bytes=VMEM_LIMIT_BYTES),
        name="diff_attn",
    )(lamv, subg, q, k, v)


def _mix_out_kernel(
        x_ref, ycs_ref, yd_ref, p_ref,
        wo_cs_ref, wo_d_ref, gffn_ref, wg_ref, wv_ref, fw_ref, fb_ref, wdown_ref,
        gple_ref, wpg_ref, wple_ref,
        o_ref,
        h_scr, gate_buf,
        *, tile):
    d_model = x_ref.shape[-1]
    d_ff = wdown_ref.shape[0]

    @pl.when(pl.program_id(1) == 0)
    def _():
        gate_buf[0:SMALL_HALO, :] = jnp.zeros((SMALL_HALO, d_ff), F32)

    def rms(v, g_ref):
        ms = jnp.sum(v * v, axis=-1, keepdims=True) * (1.0 / d_model)
        return (v * lax.rsqrt(ms + EPS) * g_ref[...]).astype(BF16)

    x1 = x_ref[0] + _dot(ycs_ref[0], wo_cs_ref[...]) + _dot(yd_ref[0], wo_d_ref[...])
    o_ref[0] = x1
    h_scr[...] = rms(x1, gffn_ref)

    hval = h_scr[...]
    base = SMALL_HALO - (FFN_CONV - 1)
    c0 = 0
    while c0 < d_ff:
        cw = min(FFN_COL_BLOCK, d_ff - c0)
        cols = slice(c0, c0 + cw)
        gate_buf[SMALL_HALO:SMALL_HALO + tile, cols] = _dot(hval, wg_ref[:, cols])
        val = _dot(hval, wv_ref[:, cols])
        gate = fb_ref[:, cols]
        for k in range(FFN_CONV):
            gate = gate + fw_ref[k:k + 1, cols] * gate_buf[base + k:base + k + tile, cols]
        hid = (_silu(gate) * val).astype(BF16)
        o_ref[0] = o_ref[0] + _dot(hid, wdown_ref[cols, :])
        c0 += cw
    gate_buf[0:SMALL_HALO, :] = gate_buf[tile:tile + SMALL_HALO, :]

    x2 = o_ref[0]
    gate = _sigmoid(_dot(rms(x2, gple_ref), wpg_ref[...]))
    o_ref[0] = x2 + _dot(p_ref[0].astype(BF16), wple_ref[...]) * gate


def _mix_out(x, ycs, ydiff, p, lp, tile):
    b, s, d = x.shape
    d_ff = lp["w_down"].shape[0]
    args = [
        lp["wo_cs"], lp["wo_d"], lp["g_ffn"], lp["w_gate"], lp["w_val"], lp["ffn_w"], lp["ffn_b"], lp["w_down"],
        lp["g_ple"], lp["w_ple_gate"], lp["w_ple"],
    ]
    seq_map = lambda i, j: (i, j, 0)
    in_specs = [
        pl.BlockSpec((1, tile, d), seq_map),
        pl.BlockSpec((1, tile, ycs.shape[-1]), seq_map),
        pl.BlockSpec((1, tile, ydiff.shape[-1]), seq_map),
        pl.BlockSpec((1, tile, p.shape[-1]), seq_map),
    ] + [_const_spec(a.shape) for a in args]
    return pl.pallas_call(
        functools.partial(_mix_out_kernel, tile=tile),
        grid=(b, s // tile),
        in_specs=in_specs,
        out_specs=pl.BlockSpec((1, tile, d), seq_map),
        out_shape=jax.ShapeDtypeStruct((b, s, d), F32),
        scratch_shapes=[
            pltpu.VMEM((tile, d), BF16),
            pltpu.VMEM((tile + SMALL_HALO, d_ff), F32),
        ],
        compiler_params=pltpu.CompilerParams(
            dimension_semantics=("parallel", "arbitrary"), vmem_limit_bytes=VMEM_LIMIT_BYTES),
        name="mix_out",
    )(x, ycs, ydiff, p, *args)


def _pad_rows(a, rows):
    return jnp.pad(a, ((0, rows - a.shape[0]), (0, 0)))


def _pad_lanes(a, lanes):
    return jnp.pad(a, ((0, 0), (0, lanes - a.shape[1])))


def _constants():
    r = jnp.arange(ROW_CHUNK)
    tril = (r[None, :] <= r[:, None]).astype(BF16)
    lane = jnp.arange(LANES)
    col = jnp.arange(SSD_W)
    expand = (lane[:, None] == (col[None, :] // SSD_HD)).astype(BF16)
    c2 = jnp.arange(DIFF_W)
    g32 = ((c2[:, None] // DIFF_HD) == (c2[None, :] // DIFF_HD)).astype(F32) / DIFF_HD
    return {"tril": tril, "triu": tril.T, "expand": expand, "g32": g32.astype(BF16)}


def _layer_params(i, w):
    d = w["w_in"].shape[1]
    w_in = w["w_in"][i]
    edges = [0]
    for sz in (2 * CONV_CH, DIFF_W, DIFF_W, DIFF_W, SSD_W, SSD_XBC, SSD_HEADS):
        edges.append(edges[-1] + sz)
    cols = [w_in[:, edges[j]:edges[j + 1]] for j in range(7)]
    w_dt = cols[6]
    row = lambda v: v[i][None, :].astype(F32)
    n_rep = DIFF_W // DIFF_HD
    w_up = w["w_up"][i]
    d_ff = w_up.shape[1] // 2
    w_out = w["w_out"][i]
    lam_init = 0.8 - 0.6 * math.exp(-0.3 * i)
    return {
        "g_mix": row(w["g_mix"]),
        "w_conv": cols[0].astype(BF16), "w_q": cols[1].astype(BF16), "w_k": cols[2].astype(BF16),
        "w_v": cols[3].astype(BF16), "w_z": cols[4].astype(BF16), "w_xbc": cols[5].astype(BF16),
        "w_dt": _pad_lanes(w_dt, LANES).astype(BF16),
        "w_dtt": _pad_rows(w_dt.T, 2 * SUBLANES).astype(BF16),
        "conv_w": _pad_rows(w["conv_dw_w"][i], 32).astype(F32),
        "conv_b": row(w["conv_dw_b"]), "conv_ln_g": row(w["conv_ln_g"]), "conv_ln_b": row(w["conv_ln_b"]),
        "gq": jnp.tile(w["q_norm_g"][i].astype(F32), n_rep)[None, :] * (DIFF_HD ** -0.5 * LOG2_E),
        "gk": jnp.tile(w["k_norm_g"][i].astype(F32), n_rep)[None, :],
        "ssd_w": _pad_rows(w["ssd_conv_w"][i], SUBLANES).astype(F32),
        "ssd_b": row(w["ssd_conv_b"]),
        "dt_bias": _pad_lanes(row(w["ssd_dt_bias"]), LANES),
        "dt_bias_t": jnp.broadcast_to(_pad_rows(w["ssd_dt_bias"][i][:, None].astype(F32), 2 * SUBLANES),
                                      (2 * SUBLANES, ROW_CHUNK)),
        "a_log": _pad_lanes(row(w["ssd_a_log"]), LANES),
        "a_log_t": jnp.broadcast_to(_pad_rows(w["ssd_a_log"][i][:, None].astype(F32), 2 * SUBLANES),
                                    (2 * SUBLANES, ROW_CHUNK)),
        "d_skip": jnp.repeat(w["ssd_d"][i].astype(F32), SSD_HD)[None, :],
        "ssd_norm_g": row(w["ssd_norm_g"]),
        "lamv": _pad_lanes(jnp.stack([w["lam_q1"][i], w["lam_k1"][i], w["lam_q2"][i], w["lam_k2"][i]]).astype(F32),
                           LANES),
        "subg": jnp.tile(w["attn_subln_g"][i].astype(F32), LANES // DIFF_VD)[None, :] * (1.0 - lam_init),
        "lam_init": lam_init,
        "wo_cs": jnp.concatenate([w_out[:CONV_CH], w_out[CONV_CH + DIFF_W:]], axis=0).astype(BF16),
        "wo_d": w_out[CONV_CH:CONV_CH + DIFF_W].astype(BF16),
        "g_ffn": row(w["g_ffn"]),
        "w_gate": w_up[:, :d_ff].astype(BF16), "w_val": w_up[:, d_ff:].astype(BF16),
        "ffn_w": _pad_rows(w["ffn_dw_w"][i], SUBLANES).astype(F32), "ffn_b": row(w["ffn_dw_b"]),
        "w_down": w["w_down"][i].astype(BF16),
        "g_ple": row(w["g_ple"]), "w_ple_gate": w["w_ple_gate"][i].astype(BF16), "w_ple": w["w_ple"][i].astype(BF16),
    }


def kernel(x, p, positions, g_mix, w_in, conv_dw_w, conv_dw_b, conv_ln_g, conv_ln_b, q_norm_g, k_norm_g, lam_q1, lam_k1, lam_q2, lam_k2, attn_subln_g, ssd_conv_w, ssd_conv_b, ssd_dt_bias, ssd_a_log, ssd_d, ssd_norm_g, w_out, g_ffn, w_up, ffn_dw_w, ffn_dw_b, w_down, g_ple, w_ple_gate, w_ple):
    w = dict(g_mix=g_mix, w_in=w_in, conv_dw_w=conv_dw_w, conv_dw_b=conv_dw_b, conv_ln_g=conv_ln_g,
             conv_ln_b=conv_ln_b, q_norm_g=q_norm_g, k_norm_g=k_norm_g, lam_q1=lam_q1, lam_k1=lam_k1,
             lam_q2=lam_q2, lam_k2=lam_k2, attn_subln_g=attn_subln_g, ssd_conv_w=ssd_conv_w,
             ssd_conv_b=ssd_conv_b, ssd_dt_bias=ssd_dt_bias, ssd_a_log=ssd_a_log, ssd_d=ssd_d,
             ssd_norm_g=ssd_norm_g, w_out=w_out, g_ffn=g_ffn, w_up=w_up, ffn_dw_w=ffn_dw_w,
             ffn_dw_b=ffn_dw_b, w_down=w_down, g_ple=g_ple, w_ple_gate=w_ple_gate, w_ple=w_ple)
    b, s, d = x.shape
    depth = w_in.shape[0]
    tile = min(s, 512)
    blk = min(s, 512)
    assert s % tile == 0 and tile % ROW_CHUNK == 0 and s % blk == 0 and blk % LANES == 0
    consts = _constants()
    cos_t, sin_t = _rope_tables(positions)
    for i in range(depth):
        lp = _layer_params(i, w)
        ycs, q, k, v = _mix_in(x, cos_t, sin_t, lp, consts, tile)
        ydiff = _diff_attn(q, k, v, lp["lamv"], lp["subg"], lp["lam_init"], blk)
        x = _mix_out(x, ycs, ydiff, p[i], lp, tile)
    return x
```

```python
import functools
import math

import jax
import jax.numpy as jnp
from jax import lax
from jax.experimental import pallas as pl
from jax.experimental.pallas import tpu as pltpu

F32 = jnp.float32
BF16 = jnp.bfloat16

CONV_CH = 256
CONV_WIDTH = 31
DIFF_HEADS = 4
DIFF_HD = 32
DIFF_VD = 64
DIFF_W = 256
SSD_HEADS = 8
SSD_HD = 64
SSD_W = 512
SSD_GROUPS = 2
SSD_STATE = 128
SSD_CONV = 4
SSD_CHUNK = 128
SSD_XBC = 1024
FFN_CONV = 3
ROPE_THETA = 10000.0
EPS = 1e-6
LOG2_E = math.log2(math.e)

LANES = 128
SUBLANES = 8
VMEM_LIMIT_BYTES = 56 * 1024 * 1024

CONV_HALO = 32
SMALL_HALO = 8
ROW_CHUNK = SSD_CHUNK
FFN_COL_BLOCK = 768


def _dot(a, b):
    return jnp.dot(a, b, preferred_element_type=F32)


def _dot_nt(a, b):
    return lax.dot_general(a, b, (((1,), (1,)), ((), ())), preferred_element_type=F32)


def _split_bf16(x, terms):
    parts = []
    r = x
    for t in range(terms):
        p = r.astype(BF16)
        parts.append(p)
        if t + 1 < terms:
            r = r - p.astype(F32)
    return parts


def _dot_exact_lhs(x, m, terms):
    parts = _split_bf16(x, terms)
    acc = _dot(parts[0], m)
    for p in parts[1:]:
        acc = acc + _dot(p, m)
    return acc


def _dot_exact_rhs(m, x, terms):
    parts = _split_bf16(x, terms)
    acc = _dot(m, parts[0])
    for p in parts[1:]:
        acc = acc + _dot(m, p)
    return acc


def _sigmoid(x):
    return 0.5 * jnp.tanh(0.5 * x) + 0.5


def _silu(x):
    h = 0.5 * x
    return h * jnp.tanh(h) + h


def _softplus(x):
    return jnp.maximum(x, 0.0) + jnp.log1p(jnp.exp(-jnp.abs(x)))


def _shift_rows(w, r):
    if r == 0:
        return w
    return pltpu.roll(w, w.shape[0] - r, axis=0)


def _causal_dwconv_rows(window, halo, width, w_ref, bias, rows):
    base = halo - (width - 1)
    acc = None
    for r in range(SUBLANES):
        taps = [o for o in range(base, base + width) if o % SUBLANES == r]
        if not taps:
            continue
        shifted = _shift_rows(window, r)
        for o in taps:
            k = o - base
            a = o - r
            term = w_ref[k:k + 1, :] * shifted[a:a + rows, :]
            acc = term if acc is None else acc + term
    return acc + bias


def _rope_kernel(pos_ref, invf_ref, sgn_ref, cos_ref, sin_ref):
    ang = pos_ref[0].astype(F32) * invf_ref[...]
    cos_ref[0] = jnp.cos(ang)
    sin_ref[0] = jnp.sin(ang) * sgn_ref[...]


def _rope_tables(positions):
    b, s = positions.shape
    t = min(s, 512)
    assert s % t == 0
    half = DIFF_HD // 2
    inv_freq = ROPE_THETA ** (-jnp.arange(0, DIFF_HD, 2, dtype=F32) / DIFF_HD)
    lane = jnp.arange(LANES)
    invf = inv_freq[(lane % DIFF_HD) % half][None, :]
    sgn = jnp.where((lane % DIFF_HD) < half, -1.0, 1.0).astype(F32)[None, :]
    return pl.pallas_call(
        _rope_kernel,
        grid=(b, s // t),
        in_specs=[
            pl.BlockSpec((1, t, 1), lambda i, j: (i, j, 0)),
            pl.BlockSpec((1, LANES), lambda i, j: (0, 0)),
            pl.BlockSpec((1, LANES), lambda i, j: (0, 0)),
        ],
        out_specs=[
            pl.BlockSpec((1, t, LANES), lambda i, j: (i, j, 0)),
            pl.BlockSpec((1, t, LANES), lambda i, j: (i, j, 0)),
        ],
        out_shape=[jax.ShapeDtypeStruct((b, s, LANES), F32)] * 2,
        compiler_params=pltpu.CompilerParams(dimension_semantics=("parallel", "parallel")),
        name="rope_tables",
    )(positions.reshape(b, s, 1), invf, sgn)


def _mix_in_kernel(
        x_ref, cos_ref, sin_ref, gmix_ref,
        win_ref, wdt_ref, wdtt_ref,
        cw_ref, cb_ref, lng_ref, lnb_ref,
        gq_ref, gk_ref, g32_ref,
        sw_ref, sb_ref, dtb_ref, dtbt_ref, alog_ref, alogt_ref, dskip_ref, ng_ref,
        tril_ref, triu_ref, exp_ref,
        ycs_ref, q_ref, k_ref, v_ref,
        hn_scr, uc_scr, q_scr, k_scr, z_scr, xbc_scr, dt_scr, dtt_scr, hbuf, xbuf, h_scr,
        *, tile):
    n_chunks = tile // ROW_CHUNK
    d_model = x_ref.shape[-1]
    edges = [0]
    for width in (2 * CONV_CH, DIFF_W, DIFF_W, DIFF_W, SSD_W, SSD_XBC):
        edges.append(edges[-1] + width)
    col = [slice(edges[j], edges[j + 1]) for j in range(6)]

    @pl.when(pl.program_id(1) == 0)
    def _():
        hbuf[0:CONV_HALO, :] = jnp.zeros((CONV_HALO, CONV_CH), F32)
        xbuf[0:SMALL_HALO, :] = jnp.zeros((SMALL_HALO, SSD_XBC), F32)
        h_scr[...] = jnp.zeros(h_scr.shape, F32)

    xr = x_ref[0]
    ms = jnp.sum(xr * xr, axis=-1, keepdims=True) * (1.0 / d_model)
    hn_scr[...] = (xr * lax.rsqrt(ms + EPS) * gmix_ref[...]).astype(BF16)

    hn = hn_scr[...]
    uc_scr[...] = _dot(hn, win_ref[:, col[0]])
    q_scr[...] = _dot(hn, win_ref[:, col[1]])
    k_scr[...] = _dot(hn, win_ref[:, col[2]])
    v_ref[0] = _dot(hn, win_ref[:, col[3]]).astype(BF16)
    z_scr[...] = _dot(hn, win_ref[:, col[4]])
    xbc_scr[...] = _dot(hn, win_ref[:, col[5]])
    dt_scr[...] = _dot(hn, wdt_ref[...])
    dtt = _dot_nt(wdtt_ref[...], hn)
    for c in range(n_chunks):
        dtt_scr[c] = dtt[:, c * ROW_CHUNK:(c + 1) * ROW_CHUNK]

    lane128 = lax.broadcasted_iota(jnp.int32, (1, LANES), 1)
    lane256 = lax.broadcasted_iota(jnp.int32, (1, 2 * LANES), 1)
    first_half = (lane256 % DIFF_HD) < (DIFF_HD // 2)
    row_i = lax.broadcasted_iota(jnp.int32, (ROW_CHUNK, ROW_CHUNK), 0)
    col_i = lax.broadcasted_iota(jnp.int32, (ROW_CHUNK, ROW_CHUNK), 1)
    causal = col_i <= row_i
    a_row = -jnp.exp(alog_ref[...])
    a_col = -jnp.exp(alogt_ref[...])

    def norm_rope(t, g_ref, cos, sin):
        ms = _dot_exact_lhs(t * t, g32_ref[...], 2)
        tn = t * lax.rsqrt(ms + EPS) * g_ref[...]
        fwd = pltpu.roll(tn, 2 * LANES - DIFF_HD // 2, axis=1)
        bwd = pltpu.roll(tn, DIFF_HD // 2, axis=1)
        rot = jnp.where(first_half, fwd, bwd)
        lo = tn[:, :LANES] * cos + rot[:, :LANES] * sin
        hi = tn[:, LANES:] * cos + rot[:, LANES:] * sin
        return jnp.concatenate([lo, hi], axis=1).astype(BF16)

    def mix_rows(c):
        r0 = c * ROW_CHUNK
        rows = pl.ds(r0, ROW_CHUNK)

        uc = uc_scr[rows, :]
        hbuf[pl.ds(r0 + CONV_HALO, ROW_CHUNK), :] = uc[:, :CONV_CH] * _sigmoid(uc[:, CONV_CH:])
        win = hbuf[pl.ds(r0, ROW_CHUNK + CONV_HALO), :]
        hc = _causal_dwconv_rows(win, CONV_HALO, CONV_WIDTH, cw_ref, cb_ref[...], ROW_CHUNK)
        mu = jnp.mean(hc, axis=-1, keepdims=True)
        xc = hc - mu
        var = jnp.mean(xc * xc, axis=-1, keepdims=True)
        yln = xc * lax.rsqrt(var + EPS) * lng_ref[...] + lnb_ref[...]
        ycs_ref[0, rows, 0:CONV_CH] = _silu(yln).astype(BF16)

        cos = cos_ref[0, rows, :]
        sin = sin_ref[0, rows, :]
        q_ref[0, rows, :] = norm_rope(q_scr[rows, :], gq_ref, cos, sin)
        k_ref[0, rows, :] = norm_rope(k_scr[rows, :], gk_ref, cos, sin)

        xbuf[pl.ds(r0 + SMALL_HALO, ROW_CHUNK), :] = xbc_scr[rows, :]
        xwin = xbuf[pl.ds(r0, ROW_CHUNK + SMALL_HALO), :]
        xbc = _silu(_causal_dwconv_rows(xwin, SMALL_HALO, SSD_CONV, sw_ref, sb_ref[...], ROW_CHUNK))
        xs = xbc[:, :SSD_W]
        bmat = xbc[:, SSD_W:SSD_W + SSD_GROUPS * SSD_STATE]
        cmat = xbc[:, SSD_W + SSD_GROUPS * SSD_STATE:]

        dt = _softplus(dt_scr[rows, :] + dtb_ref[...])
        cs = _dot_exact_rhs(tril_ref[...], dt * a_row, 3)
        dt_t = _softplus(dtt_scr[c] + dtbt_ref[...])
        cs_t = _dot_exact_lhs(dt_t * a_col, triu_ref[...], 3)

        dt_w = _dot_exact_lhs(dt, exp_ref[...], 2)
        cs_w = _dot_exact_lhs(cs, exp_ref[...], 3)
        cs_last = cs_w[ROW_CHUNK - 1:ROW_CHUNK, :]
        xdt = xs * dt_w
        xdt_b = xdt.astype(BF16)
        xdec_b = (xdt * jnp.exp(cs_last - cs_w)).astype(BF16)
        ecs = jnp.exp(cs_w)
        chunk_decay = jnp.exp(cs_last)
        z = z_scr[rows, :]

        gw = SSD_W // SSD_GROUPS
        for g in range(SSD_GROUPS):
            bg = bmat[:, g * SSD_STATE:(g + 1) * SSD_STATE]
            cg_b = cmat[:, g * SSD_STATE:(g + 1) * SSD_STATE].astype(BF16)
            cb = _dot_nt(cg_b, bg.astype(BF16))
            yd_parts = []
            for pr in range(gw // LANES):
                xpair = xdt_b[:, g * gw + pr * LANES:g * gw + (pr + 1) * LANES]
                outs = []
                for hh in range(LANES // SSD_HD):
                    h = (g * gw + pr * LANES) // SSD_HD + hh
                    seg = cs[:, h:h + 1] - cs_t[h:h + 1, :]
                    lmat = jnp.exp(jnp.where(causal, seg, -jnp.inf))
                    outs.append(_dot((cb * lmat).astype(BF16), xpair))
                yd_parts.append(jnp.where(lane128 < SSD_HD, outs[0], outs[1]))
            yd = jnp.concatenate(yd_parts, axis=1)
            hstate = h_scr[g]
            yo = _dot(cg_b, hstate.astype(BF16)) * ecs[:, g * gw:(g + 1) * gw]
            st = _dot(bg.T.astype(BF16), xdec_b[:, g * gw:(g + 1) * gw])
            h_scr[g] = hstate * chunk_decay[:, g * gw:(g + 1) * gw] + st
            yg = (yd + yo + xs[:, g * gw:(g + 1) * gw] * dskip_ref[:, g * gw:(g + 1) * gw])
            yg = yg * _silu(z[:, g * gw:(g + 1) * gw])
            ms = jnp.mean(yg * yg, axis=-1, keepdims=True)
            yn = yg * lax.rsqrt(ms + EPS) * ng_ref[:, g * gw:(g + 1) * gw]
            ycs_ref[0, rows, CONV_CH + g * gw:CONV_CH + (g + 1) * gw] = yn.astype(BF16)

    for c in range(n_chunks):
        mix_rows(c)

    hbuf[0:CONV_HALO, :] = hbuf[tile:tile + CONV_HALO, :]
    xbuf[0:SMALL_HALO, :] = xbuf[tile:tile + SMALL_HALO, :]


def _const_spec(shape):
    nd = len(shape)
    return pl.BlockSpec(shape, lambda *_: (0,) * nd, pipeline_mode=pl.Buffered(1))


def _layer_spec(block, index):
    return pl.BlockSpec((None,) + tuple(block), lambda *_: tuple(index), pipeline_mode=pl.Buffered(1))


def _mix_in(x, cos_t, sin_t, layer, wts, lp, consts, tile):
    b, s, d = x.shape
    proj_cols = 2 * CONV_CH + 3 * DIFF_W + SSD_W + SSD_XBC
    small = [
        lp["w_dt"], lp["w_dtt"],
        lp["conv_w"], lp["conv_b"], lp["conv_ln_g"], lp["conv_ln_b"],
        lp["gq"], lp["gk"], consts["g32"],
        lp["ssd_w"], lp["ssd_b"], lp["dt_bias"], lp["dt_bias_t"], lp["a_log"], lp["a_log_t"],
        lp["d_skip"], lp["ssd_norm_g"],
        consts["tril"], consts["triu"], consts["expand"],
    ]
    seq_map = lambda i, j: (i, j, 0)
    in_specs = [
        pl.BlockSpec((1, tile, d), seq_map),
        pl.BlockSpec((1, tile, LANES), seq_map),
        pl.BlockSpec((1, tile, LANES), seq_map),
        _const_spec(lp["g_mix"].shape),
        _layer_spec((d, proj_cols), (layer, 0, 0)),
    ] + [_const_spec(a.shape) for a in small]
    out_shape = [
        jax.ShapeDtypeStruct((b, s, CONV_CH + SSD_W), BF16),
        jax.ShapeDtypeStruct((b, s, DIFF_W), BF16),
        jax.ShapeDtypeStruct((b, s, DIFF_W), BF16),
        jax.ShapeDtypeStruct((b, s, DIFF_W), BF16),
    ]
    out_specs = [
        pl.BlockSpec((1, tile, CONV_CH + SSD_W), seq_map),
        pl.BlockSpec((1, tile, DIFF_W), seq_map),
        pl.BlockSpec((1, tile, DIFF_W), seq_map),
        pl.BlockSpec((1, tile, DIFF_W), seq_map),
    ]
    scratch = [
        pltpu.VMEM((tile, d), BF16),
        pltpu.VMEM((tile, 2 * CONV_CH), F32),
        pltpu.VMEM((tile, DIFF_W), F32),
        pltpu.VMEM((tile, DIFF_W), F32),
        pltpu.VMEM((tile, SSD_W), F32),
        pltpu.VMEM((tile, SSD_XBC), F32),
        pltpu.VMEM((tile, LANES), F32),
        pltpu.VMEM((tile // ROW_CHUNK, 2 * SUBLANES, ROW_CHUNK), F32),
        pltpu.VMEM((tile + CONV_HALO, CONV_CH), F32),
        pltpu.VMEM((tile + SMALL_HALO, SSD_XBC), F32),
        pltpu.VMEM((SSD_GROUPS, SSD_STATE, SSD_W // SSD_GROUPS), F32),
    ]
    return pl.pallas_call(
        functools.partial(_mix_in_kernel, tile=tile),
        grid=(b, s // tile),
        in_specs=in_specs,
        out_specs=out_specs,
        out_shape=out_shape,
        scratch_shapes=scratch,
        compiler_params=pltpu.CompilerParams(
            dimension_semantics=("parallel", "arbitrary"), vmem_limit_bytes=VMEM_LIMIT_BYTES),
        name="mix_in",
    )(x, cos_t, sin_t, lp["g_mix"], wts["w_in"], *small)


def _attn_kernel(lamv_ref, subg_ref, q_ref, k_ref, v_ref, o_ref, vext, q4_scr, acc_scr, m_scr, sa, sb,
                 *, blk, lam_init):
    i = pl.program_id(2)
    n_heads = LANES // DIFF_VD
    n_maps = 2 * n_heads
    rows = n_maps * blk
    lane = lax.broadcasted_iota(jnp.int32, (1, LANES), 1)
    lo_half = lane < DIFF_VD

    @pl.when(i == 0)
    def _():
        vb = v_ref[0]
        one = jnp.ones_like(vb)
        vext[0] = jnp.where(lo_half, vb, one)
        vext[1] = jnp.where(lo_half, one, vb)

    qb = q_ref[0]
    for j in range(n_maps):
        keep = (lane >= j * DIFF_HD) & (lane < (j + 1) * DIFF_HD)
        q4_scr[j * blk:(j + 1) * blk, :] = jnp.where(keep, qb, jnp.zeros_like(qb))
    m_scr[...] = jnp.full((rows, LANES), -jnp.inf, F32)
    acc_scr[...] = jnp.zeros((rows, LANES), F32)

    row_i = lax.broadcasted_iota(jnp.int32, (blk, blk), 0)
    col_i = lax.broadcasted_iota(jnp.int32, (blk, blk), 1)
    causal = col_i <= row_i

    def scores(kb, s_ref):
        k0 = pl.multiple_of(kb * blk, blk)
        s_ref[...] = _dot_nt(q4_scr[...], k_ref[0, pl.ds(k0, blk), :])

    def consume(kb, s_ref, masked):
        k0 = pl.multiple_of(kb * blk, blk)
        s = s_ref[...]
        if masked:
            s = jnp.concatenate(
                [jnp.where(causal, s[j * blk:(j + 1) * blk, :], -jnp.inf) for j in range(n_maps)], axis=0)
        cols = [s[:, c * LANES:(c + 1) * LANES] for c in range(blk // LANES)]
        rm = cols[0]
        for cc in cols[1:]:
            rm = jnp.maximum(rm, cc)
        m_old = m_scr[...]
        m_new = jnp.maximum(m_old, jnp.max(rm, axis=-1, keepdims=True))
        alpha = jnp.exp2(m_old - m_new)
        p = jnp.concatenate([jnp.exp2(cc - m_new) for cc in cols], axis=1).astype(BF16)
        hr = rows // n_heads
        pv = jnp.concatenate(
            [_dot(p[h * hr:(h + 1) * hr, :], vext[h, pl.ds(k0, blk), :]) for h in range(n_heads)], axis=0)
        acc_scr[...] = acc_scr[...] * alpha + pv
        m_scr[...] = m_new

    scores(0, sa)

    def pair(t, carry):
        scores(2 * t + 1, sb)
        consume(2 * t, sa, False)
        scores(2 * t + 2, sa)
        consume(2 * t + 1, sb, False)
        return carry

    lax.fori_loop(0, i // 2, pair, 0)

    @pl.when(i % 2 == 0)
    def _():
        consume(i, sa, True)

    @pl.when(i % 2 == 1)
    def _():
        scores(i, sb)
        consume(i - 1, sa, False)
        consume(i, sb, True)

    lv = lamv_ref[...]
    lam = (jnp.exp(jnp.sum(lv[0:1] * lv[1:2], axis=-1, keepdims=True))
           - jnp.exp(jnp.sum(lv[2:3] * lv[3:4], axis=-1, keepdims=True)) + lam_init)
    outs = []
    for h in range(n_heads):
        a1 = acc_scr[(2 * h) * blk:(2 * h + 1) * blk, :]
        a2 = acc_scr[(2 * h + 1) * blk:(2 * h + 2) * blk, :]
        r1 = a1 / pltpu.roll(a1, DIFF_VD, axis=1)
        r2 = a2 / pltpu.roll(a2, DIFF_VD, axis=1)
        outs.append(r1 - lam * r2)
    o = jnp.where(lo_half, outs[0], outs[1])
    o2 = o * o
    ms_lo = jnp.sum(jnp.where(lo_half, o2, 0.0), axis=-1, keepdims=True)
    ms_hi = jnp.sum(jnp.where(lo_half, 0.0, o2), axis=-1, keepdims=True)
    ms = jnp.where(lo_half, ms_lo, ms_hi) * (1.0 / DIFF_VD)
    o_ref[0] = (o * lax.rsqrt(ms + EPS) * subg_ref[...]).astype(BF16)


def _diff_attn(q, k, v, lamv, subg, lam_init, blk):
    b, s, w = q.shape
    n_pairs = w // LANES
    n_heads = LANES // DIFF_VD
    return pl.pallas_call(
        functools.partial(_attn_kernel, blk=blk, lam_init=lam_init),
        grid=(b, n_pairs, s // blk),
        in_specs=[
            pl.BlockSpec((4, LANES), lambda bi, pi, i: (0, 0)),
            pl.BlockSpec((1, LANES), lambda bi, pi, i: (0, 0)),
            pl.BlockSpec((1, blk, LANES), lambda bi, pi, i: (bi, i, pi)),
            pl.BlockSpec((1, s, LANES), lambda bi, pi, i: (bi, 0, pi)),
            pl.BlockSpec((1, s, LANES), lambda bi, pi, i: (bi, 0, pi)),
        ],
        out_specs=pl.BlockSpec((1, blk, LANES), lambda bi, pi, i: (bi, i, pi)),
        out_shape=jax.ShapeDtypeStruct((b, s, w), BF16),
        scratch_shapes=[
            pltpu.VMEM((n_heads, s, LANES), BF16),
            pltpu.VMEM((2 * n_heads * blk, LANES), BF16),
            pltpu.VMEM((2 * n_heads * blk, LANES), F32),
            pltpu.VMEM((2 * n_heads * blk, LANES), F32),
            pltpu.VMEM((2 * n_heads * blk, blk), F32),
            pltpu.VMEM((2 * n_heads * blk, blk), F32),
        ],
        compiler_params=pltpu.CompilerParams(
            dimension_semantics=("parallel", "parallel", "arbitrary"), vmem_limit_bytes=VMEM_LIMIT_BYTES),
        name="diff_attn",
    )(lamv, subg, q, k, v)


def _mix_out_kernel(
        x_ref, ycs_ref, yd_ref, p_ref,
        wo_c_ref, wo_d_ref, wo_s_ref, gffn_ref, wg_ref, wv_ref, fw_ref, fb_ref, wdown_ref,
        gple_ref, wpg_ref, wple_ref,
        o_ref,
        h_scr, gate_buf,
        *, tile):
    d_model = x_ref.shape[-1]
    d_ff = wdown_ref.shape[0]

    @pl.when(pl.program_id(1) == 0)
    def _():
        gate_buf[0:SMALL_HALO, :] = jnp.zeros((SMALL_HALO, d_ff), F32)

    def rms(v, g_ref):
        ms = jnp.sum(v * v, axis=-1, keepdims=True) * (1.0 / d_model)
        return (v * lax.rsqrt(ms + EPS) * g_ref[...]).astype(BF16)

    x1 = (x_ref[0] + _dot(ycs_ref[0, :, 0:CONV_CH], wo_c_ref[...]) + _dot(yd_ref[0], wo_d_ref[...])
          + _dot(ycs_ref[0, :, CONV_CH:], wo_s_ref[...]))
    o_ref[0] = x1
    h_scr[...] = rms(x1, gffn_ref)

    hval = h_scr[...]
    base = SMALL_HALO - (FFN_CONV - 1)
    c0 = 0
    while c0 < d_ff:
        cw = min(FFN_COL_BLOCK, d_ff - c0)
        cols = slice(c0, c0 + cw)
        gate_buf[SMALL_HALO:SMALL_HALO + tile, cols] = _dot(hval, wg_ref[:, cols])
        val = _dot(hval, wv_ref[:, cols])
        gate = fb_ref[:, cols]
        for k in range(FFN_CONV):
            gate = gate + fw_ref[k:k + 1, cols] * gate_buf[base + k:base + k + tile, cols]
        hid = (_silu(gate) * val).astype(BF16)
        o_ref[0] = o_ref[0] + _dot(hid, wdown_ref[cols, :])
        c0 += cw
    gate_buf[0:SMALL_HALO, :] = gate_buf[tile:tile + SMALL_HALO, :]

    x2 = o_ref[0]
    gate = _sigmoid(_dot(rms(x2, gple_ref), wpg_ref[...]))
    o_ref[0] = x2 + _dot(p_ref[0].astype(BF16), wple_ref[...]) * gate


def _mix_out(x, ycs, ydiff, p, layer, wts, lp, tile):
    b, s, d = x.shape
    d_ff = wts["w_down"].shape[1]
    d_ple = p.shape[-1]
    seq_map = lambda i, j: (i, j, 0)
    in_specs = [
        pl.BlockSpec((1, tile, d), seq_map),
        pl.BlockSpec((1, tile, ycs.shape[-1]), seq_map),
        pl.BlockSpec((1, tile, ydiff.shape[-1]), seq_map),
        pl.BlockSpec((None, 1, tile, d_ple), lambda i, j: (layer, i, j, 0)),
        _layer_spec((CONV_CH, d), (layer, 0, 0)),
        _layer_spec((DIFF_W, d), (layer, CONV_CH // DIFF_W, 0)),
        _layer_spec((SSD_W, d), (layer, (CONV_CH + DIFF_W) // SSD_W, 0)),
        _const_spec(lp["g_ffn"].shape),
        _layer_spec((d, d_ff), (layer, 0, 0)),
        _layer_spec((d, d_ff), (layer, 0, 1)),
        _const_spec(lp["ffn_w"].shape), _const_spec(lp["ffn_b"].shape),
        _layer_spec((d_ff, d), (layer, 0, 0)),
        _const_spec(lp["g_ple"].shape),
        _layer_spec((d, d), (layer, 0, 0)),
        _layer_spec((d_ple, d), (layer, 0, 0)),
    ]
    return pl.pallas_call(
        functools.partial(_mix_out_kernel, tile=tile),
        grid=(b, s // tile),
        in_specs=in_specs,
        out_specs=pl.BlockSpec((1, tile, d), seq_map),
        out_shape=jax.ShapeDtypeStruct((b, s, d), F32),
        scratch_shapes=[
            pltpu.VMEM((tile, d), BF16),
            pltpu.VMEM((tile + SMALL_HALO, d_ff), F32),
        ],
        compiler_params=pltpu.CompilerParams(
            dimension_semantics=("parallel", "arbitrary"), vmem_limit_bytes=VMEM_LIMIT_BYTES),
        name="mix_out",
    )(x, ycs, ydiff, p, wts["w_out"], wts["w_out"], wts["w_out"], lp["g_ffn"], wts["w_up"], wts["w_up"],
      lp["ffn_w"], lp["ffn_b"], wts["w_down"], lp["g_ple"], wts["w_ple_gate"], wts["w_ple"])


def _pad_rows(a, rows):
    return jnp.pad(a, ((0, rows - a.shape[0]), (0, 0)))


def _pad_lanes(a, lanes):
    return jnp.pad(a, ((0, 0), (0, lanes - a.shape[1])))


def _constants():
    r = jnp.arange(ROW_CHUNK)
    tril = (r[None, :] <= r[:, None]).astype(BF16)
    lane = jnp.arange(LANES)
    col = jnp.arange(SSD_W)
    expand = (lane[:, None] == (col[None, :] // SSD_HD)).astype(BF16)
    c2 = jnp.arange(DIFF_W)
    g32 = ((c2[:, None] // DIFF_HD) == (c2[None, :] // DIFF_HD)).astype(F32) / DIFF_HD
    return {"tril": tril, "triu": tril.T, "expand": expand, "g32": g32.astype(BF16)}


def _layer_params(i, w):
    proj_cols = 2 * CONV_CH + 3 * DIFF_W + SSD_W + SSD_XBC
    w_dt = w["w_in"][i][:, proj_cols:]
    row = lambda v: v[i][None, :].astype(F32)
    n_rep = DIFF_W // DIFF_HD
    lam_init = 0.8 - 0.6 * math.exp(-0.3 * i)
    return {
        "g_mix": row(w["g_mix"]),
        "w_dt": _pad_lanes(w_dt, LANES).astype(BF16),
        "w_dtt": _pad_rows(w_dt.T, 2 * SUBLANES).astype(BF16),
        "conv_w": _pad_rows(w["conv_dw_w"][i], 32).astype(F32),
        "conv_b": row(w["conv_dw_b"]), "conv_ln_g": row(w["conv_ln_g"]), "conv_ln_b": row(w["conv_ln_b"]),
        "gq": jnp.tile(w["q_norm_g"][i].astype(F32), n_rep)[None, :] * (DIFF_HD ** -0.5 * LOG2_E),
        "gk": jnp.tile(w["k_norm_g"][i].astype(F32), n_rep)[None, :],
        "ssd_w": _pad_rows(w["ssd_conv_w"][i], SUBLANES).astype(F32),
        "ssd_b": row(w["ssd_conv_b"]),
        "dt_bias": _pad_lanes(row(w["ssd_dt_bias"]), LANES),
        "dt_bias_t": jnp.broadcast_to(_pad_rows(w["ssd_dt_bias"][i][:, None].astype(F32), 2 * SUBLANES),
                                      (2 * SUBLANES, ROW_CHUNK)),
        "a_log": _pad_lanes(row(w["ssd_a_log"]), LANES),
        "a_log_t": jnp.broadcast_to(_pad_rows(w["ssd_a_log"][i][:, None].astype(F32), 2 * SUBLANES),
                                    (2 * SUBLANES, ROW_CHUNK)),
        "d_skip": jnp.repeat(w["ssd_d"][i].astype(F32), SSD_HD)[None, :],
        "ssd_norm_g": row(w["ssd_norm_g"]),
        "lamv": _pad_lanes(jnp.stack([w["lam_q1"][i], w["lam_k1"][i], w["lam_q2"][i], w["lam_k2"][i]]).astype(F32),
                           LANES),
        "subg": jnp.tile(w["attn_subln_g"][i].astype(F32), LANES // DIFF_VD)[None, :] * (1.0 - lam_init),
        "lam_init": lam_init,
        "g_ffn": row(w["g_ffn"]),
        "ffn_w": _pad_rows(w["ffn_dw_w"][i], SUBLANES).astype(F32), "ffn_b": row(w["ffn_dw_b"]),
        "g_ple": row(w["g_ple"]),
    }


def kernel(x, p, positions, g_mix, w_in, conv_dw_w, conv_dw_b, conv_ln_g, conv_ln_b, q_norm_g, k_norm_g, lam_q1, lam_k1, lam_q2, lam_k2, attn_subln_g, ssd_conv_w, ssd_conv_b, ssd_dt_bias, ssd_a_log, ssd_d, ssd_norm_g, w_out, g_ffn, w_up, ffn_dw_w, ffn_dw_b, w_down, g_ple, w_ple_gate, w_ple):
    w = dict(g_mix=g_mix, w_in=w_in, conv_dw_w=conv_dw_w, conv_dw_b=conv_dw_b, conv_ln_g=conv_ln_g,
             conv_ln_b=conv_ln_b, q_norm_g=q_norm_g, k_norm_g=k_norm_g, lam_q1=lam_q1, lam_k1=lam_k1,
             lam_q2=lam_q2, lam_k2=lam_k2, attn_subln_g=attn_subln_g, ssd_conv_w=ssd_conv_w,
             ssd_conv_b=ssd_conv_b, ssd_dt_bias=ssd_dt_bias, ssd_a_log=ssd_a_log, ssd_d=ssd_d,
             ssd_norm_g=ssd_norm_g, g_ffn=g_ffn, ffn_dw_w=ffn_dw_w, ffn_dw_b=ffn_dw_b, g_ple=g_ple)
    wts = {"w_in": w_in.astype(BF16), "w_out": w_out.astype(BF16), "w_up": w_up.astype(BF16),
           "w_down": w_down.astype(BF16), "w_ple_gate": w_ple_gate.astype(BF16), "w_ple": w_ple.astype(BF16)}
    b, s, d = x.shape
    depth = w_in.shape[0]
    tile = min(s, 512)
    blk = min(s, 512)
    assert s % tile == 0 and tile % ROW_CHUNK == 0 and s % blk == 0 and blk % LANES == 0
    assert w_up.shape[-1] % (2 * LANES) == 0 and CONV_CH == DIFF_W and (CONV_CH + DIFF_W) == SSD_W
    consts = _constants()
    cos_t, sin_t = _rope_tables(positions)
    for i in range(depth):
        lp = _layer_params(i, w)
        ycs, q, k, v = _mix_in(x, cos_t, sin_t, i, wts, lp, consts, tile)
        ydiff = _diff_attn(q, k, v, lp["lamv"], lp["subg"], lp["lam_init"], blk)
        x = _mix_out(x, ycs, ydiff, p, i, wts, lp, tile)
    return x
```

```python
import functools
import math

import jax
import jax.numpy as jnp
from jax import lax
from jax.experimental import pallas as pl
from jax.experimental.pallas import tpu as pltpu

F32 = jnp.float32
BF16 = jnp.bfloat16

CONV_CH = 256
CONV_WIDTH = 31
DIFF_HEADS = 4
DIFF_HD = 32
DIFF_VD = 64
DIFF_W = 256
SSD_HEADS = 8
SSD_HD = 64
SSD_W = 512
SSD_GROUPS = 2
SSD_STATE = 128
SSD_CONV = 4
SSD_CHUNK = 128
SSD_XBC = 1024
FFN_CONV = 3
ROPE_THETA = 10000.0
EPS = 1e-6
LOG2_E = math.log2(math.e)

LANES = 128
SUBLANES = 8
VMEM_LIMIT_BYTES = 56 * 1024 * 1024

CONV_HALO = 32
SMALL_HALO = 8
ROW_CHUNK = SSD_CHUNK


def _dot(a, b):
    return jnp.dot(a, b, preferred_element_type=F32)


def _dot_nt(a, b):
    return lax.dot_general(a, b, (((1,), (1,)), ((), ())), preferred_element_type=F32)


def _split_bf16(x, terms):
    parts = []
    r = x
    for t in range(terms):
        p = r.astype(BF16)
        parts.append(p)
        if t + 1 < terms:
            r = r - p.astype(F32)
    return parts


def _dot_exact_lhs(x, m, terms):
    parts = _split_bf16(x, terms)
    acc = _dot(parts[0], m)
    for p in parts[1:]:
        acc = acc + _dot(p, m)
    return acc


def _dot_exact_rhs(m, x, terms):
    parts = _split_bf16(x, terms)
    acc = _dot(m, parts[0])
    for p in parts[1:]:
        acc = acc + _dot(m, p)
    return acc


def _sigmoid(x):
    return 0.5 * jnp.tanh(0.5 * x) + 0.5


def _silu(x):
    h = 0.5 * x
    return h * jnp.tanh(h) + h


def _softplus(x):
    return jnp.maximum(x, 0.0) + jnp.log1p(jnp.exp(-jnp.abs(x)))


def _shift_rows(w, r):
    if r == 0:
        return w
    return pltpu.roll(w, w.shape[0] - r, axis=0)


def _causal_dwconv_rows(window, halo, width, w_ref, bias, rows):
    base = halo - (width - 1)
    acc = None
    for r in range(SUBLANES):
        taps = [o for o in range(base, base + width) if o % SUBLANES == r]
        if not taps:
            continue
        shifted = _shift_rows(window, r)
        for o in taps:
            k = o - base
            a = o - r
            term = w_ref[k:k + 1, :] * shifted[a:a + rows, :]
            acc = term if acc is None else acc + term
    return acc + bias


def _rope_kernel(pos_ref, invf_ref, sgn_ref, cos_ref, sin_ref):
    ang = pos_ref[0].astype(F32) * invf_ref[...]
    cos_ref[0] = jnp.cos(ang)
    sin_ref[0] = jnp.sin(ang) * sgn_ref[...]


def _rope_tables(positions):
    b, s = positions.shape
    t = min(s, 512)
    assert s % t == 0
    half = DIFF_HD // 2
    inv_freq = ROPE_THETA ** (-jnp.arange(0, DIFF_HD, 2, dtype=F32) / DIFF_HD)
    lane = jnp.arange(LANES)
    invf = inv_freq[(lane % DIFF_HD) % half][None, :]
    sgn = jnp.where((lane % DIFF_HD) < half, -1.0, 1.0).astype(F32)[None, :]
    return pl.pallas_call(
        _rope_kernel,
        grid=(b, s // t),
        in_specs=[
            pl.BlockSpec((1, t, 1), lambda i, j: (i, j, 0)),
            pl.BlockSpec((1, LANES), lambda i, j: (0, 0)),
            pl.BlockSpec((1, LANES), lambda i, j: (0, 0)),
        ],
        out_specs=[
            pl.BlockSpec((1, t, LANES), lambda i, j: (i, j, 0)),
            pl.BlockSpec((1, t, LANES), lambda i, j: (i, j, 0)),
        ],
        out_shape=[jax.ShapeDtypeStruct((b, s, LANES), F32)] * 2,
        compiler_params=pltpu.CompilerParams(dimension_semantics=("parallel", "parallel")),
        name="rope_tables",
    )(positions.reshape(b, s, 1), invf, sgn)


def _mix_in_kernel(
        x_ref, cos_ref, sin_ref, gmix_ref,
        win_ref, wdt_ref, wdtt_ref,
        cw_ref, cb_ref, lng_ref, lnb_ref,
        gq_ref, gk_ref, g32_ref,
        sw_ref, sb_ref, dtb_ref, dtbt_ref, alog_ref, alogt_ref, dskip_ref, ng_ref,
        tril_ref, triu_ref, exp_ref,
        ycs_ref, q_ref, k_ref, v_ref,
        hn_scr, uc_scr, q_scr, k_scr, z_scr, xbc_scr, dt_scr, dtt_scr, hbuf, xbuf, h_scr,
        *, tile):
    n_chunks = tile // ROW_CHUNK
    d_model = x_ref.shape[-1]
    edges = [0]
    for width in (2 * CONV_CH, DIFF_W, DIFF_W, DIFF_W, SSD_W, SSD_XBC):
        edges.append(edges[-1] + width)
    col = [slice(edges[j], edges[j + 1]) for j in range(6)]

    @pl.when(pl.program_id(1) == 0)
    def _():
        hbuf[0:CONV_HALO, :] = jnp.zeros((CONV_HALO, CONV_CH), F32)
        xbuf[0:SMALL_HALO, :] = jnp.zeros((SMALL_HALO, SSD_XBC), F32)
        h_scr[...] = jnp.zeros(h_scr.shape, F32)

    xr = x_ref[0]
    ms = jnp.sum(xr * xr, axis=-1, keepdims=True) * (1.0 / d_model)
    hn_scr[...] = (xr * lax.rsqrt(ms + EPS) * gmix_ref[...]).astype(BF16)

    hn = hn_scr[...]
    uc_scr[...] = _dot(hn, win_ref[:, col[0]])
    q_scr[...] = _dot(hn, win_ref[:, col[1]])
    k_scr[...] = _dot(hn, win_ref[:, col[2]])
    v_ref[0] = _dot(hn, win_ref[:, col[3]]).astype(BF16)
    z_scr[...] = _dot(hn, win_ref[:, col[4]])
    xbc_scr[...] = _dot(hn, win_ref[:, col[5]])
    dt_scr[...] = _dot(hn, wdt_ref[...])
    dtt = _dot_nt(wdtt_ref[...], hn)
    for c in range(n_chunks):
        dtt_scr[c] = dtt[:, c * ROW_CHUNK:(c + 1) * ROW_CHUNK]

    lane128 = lax.broadcasted_iota(jnp.int32, (1, LANES), 1)
    lane256 = lax.broadcasted_iota(jnp.int32, (1, 2 * LANES), 1)
    first_half = (lane256 % DIFF_HD) < (DIFF_HD // 2)
    row_i = lax.broadcasted_iota(jnp.int32, (ROW_CHUNK, ROW_CHUNK), 0)
    col_i = lax.broadcasted_iota(jnp.int32, (ROW_CHUNK, ROW_CHUNK), 1)
    causal = col_i <= row_i
    a_row = -jnp.exp(alog_ref[...])
    a_col = -jnp.exp(alogt_ref[...])

    def norm_rope(t, g_ref, cos, sin):
        ms = _dot_exact_lhs(t * t, g32_ref[...], 2)
        tn = t * lax.rsqrt(ms + EPS) * g_ref[...]
        fwd = pltpu.roll(tn, 2 * LANES - DIFF_HD // 2, axis=1)
        bwd = pltpu.roll(tn, DIFF_HD // 2, axis=1)
        rot = jnp.where(first_half, fwd, bwd)
        lo = tn[:, :LANES] * cos + rot[:, :LANES] * sin
        hi = tn[:, LANES:] * cos + rot[:, LANES:] * sin
        return jnp.concatenate([lo, hi], axis=1).astype(BF16)

    def mix_rows(c):
        r0 = c * ROW_CHUNK
        rows = pl.ds(r0, ROW_CHUNK)

        uc = uc_scr[rows, :]
        hbuf[pl.ds(r0 + CONV_HALO, ROW_CHUNK), :] = uc[:, :CONV_CH] * _sigmoid(uc[:, CONV_CH:])
        win = hbuf[pl.ds(r0, ROW_CHUNK + CONV_HALO), :]
        hc = _causal_dwconv_rows(win, CONV_HALO, CONV_WIDTH, cw_ref, cb_ref[...], ROW_CHUNK)
        mu = jnp.mean(hc, axis=-1, keepdims=True)
        xc = hc - mu
        var = jnp.mean(xc * xc, axis=-1, keepdims=True)
        yln = xc * lax.rsqrt(var + EPS) * lng_ref[...] + lnb_ref[...]
        ycs_ref[0, rows, 0:CONV_CH] = _silu(yln).astype(BF16)

        cos = cos_ref[0, rows, :]
        sin = sin_ref[0, rows, :]
        q_ref[0, rows, :] = norm_rope(q_scr[rows, :], gq_ref, cos, sin)
        k_ref[0, rows, :] = norm_rope(k_scr[rows, :], gk_ref, cos, sin)

        xbuf[pl.ds(r0 + SMALL_HALO, ROW_CHUNK), :] = xbc_scr[rows, :]
        xwin = xbuf[pl.ds(r0, ROW_CHUNK + SMALL_HALO), :]
        xbc = _silu(_causal_dwconv_rows(xwin, SMALL_HALO, SSD_CONV, sw_ref, sb_ref[...], ROW_CHUNK))
        xs = xbc[:, :SSD_W]
        bmat = xbc[:, SSD_W:SSD_W + SSD_GROUPS * SSD_STATE]
        cmat = xbc[:, SSD_W + SSD_GROUPS * SSD_STATE:]

        dt = _softplus(dt_scr[rows, :] + dtb_ref[...])
        cs = _dot_exact_rhs(tril_ref[...], dt * a_row, 3)
        dt_t = _softplus(dtt_scr[c] + dtbt_ref[...])
        cs_t = _dot_exact_lhs(dt_t * a_col, triu_ref[...], 3)

        dt_w = _dot_exact_lhs(dt, exp_ref[...], 2)
        cs_w = _dot_exact_lhs(cs, exp_ref[...], 3)
        cs_last = cs_w[ROW_CHUNK - 1:ROW_CHUNK, :]
        xdt = xs * dt_w
        xdt_b = xdt.astype(BF16)
        xdec_b = (xdt * jnp.exp(cs_last - cs_w)).astype(BF16)
        ecs = jnp.exp(cs_w)
        chunk_decay = jnp.exp(cs_last)
        z = z_scr[rows, :]

        gw = SSD_W // SSD_GROUPS
        for g in range(SSD_GROUPS):
            bg = bmat[:, g * SSD_STATE:(g + 1) * SSD_STATE]
            cg_b = cmat[:, g * SSD_STATE:(g + 1) * SSD_STATE].astype(BF16)
            cb = _dot_nt(cg_b, bg.astype(BF16))
            yd_parts = []
            for pr in range(gw // LANES):
                xpair = xdt_b[:, g * gw + pr * LANES:g * gw + (pr + 1) * LANES]
                outs = []
                for hh in range(LANES // SSD_HD):
                    h = (g * gw + pr * LANES) // SSD_HD + hh
                    seg = cs[:, h:h + 1] - cs_t[h:h + 1, :]
                    lmat = jnp.exp(jnp.where(causal, seg, -jnp.inf))
                    outs.append(_dot((cb * lmat).astype(BF16), xpair))
                yd_parts.append(jnp.where(lane128 < SSD_HD, outs[0], outs[1]))
            yd = jnp.concatenate(yd_parts, axis=1)
            hstate = h_scr[g]
            yo = _dot(cg_b, hstate.astype(BF16)) * ecs[:, g * gw:(g + 1) * gw]
            st = _dot(bg.T.astype(BF16), xdec_b[:, g * gw:(g + 1) * gw])
            h_scr[g] = hstate * chunk_decay[:, g * gw:(g + 1) * gw] + st
            yg = (yd + yo + xs[:, g * gw:(g + 1) * gw] * dskip_ref[:, g * gw:(g + 1) * gw])
            yg = yg * _silu(z[:, g * gw:(g + 1) * gw])
            ms = jnp.mean(yg * yg, axis=-1, keepdims=True)
            yn = yg * lax.rsqrt(ms + EPS) * ng_ref[:, g * gw:(g + 1) * gw]
            ycs_ref[0, rows, CONV_CH + g * gw:CONV_CH + (g + 1) * gw] = yn.astype(BF16)

    for c in range(n_chunks):
        mix_rows(c)

    hbuf[0:CONV_HALO, :] = hbuf[tile:tile + CONV_HALO, :]
    xbuf[0:SMALL_HALO, :] = xbuf[tile:tile + SMALL_HALO, :]


def _const_spec(shape):
    nd = len(shape)
    return pl.BlockSpec(shape, lambda *_: (0,) * nd, pipeline_mode=pl.Buffered(1))


def _layer_spec(block, index):
    return pl.BlockSpec((None,) + tuple(block), lambda *_: tuple(index), pipeline_mode=pl.Buffered(1))


def _mix_in(x, cos_t, sin_t, layer, wts, lp, consts, tile):
    b, s, d = x.shape
    proj_cols = 2 * CONV_CH + 3 * DIFF_W + SSD_W + SSD_XBC
    small = [
        lp["w_dt"], lp["w_dtt"],
        lp["conv_w"], lp["conv_b"], lp["conv_ln_g"], lp["conv_ln_b"],
        lp["gq"], lp["gk"], consts["g32"],
        lp["ssd_w"], lp["ssd_b"], lp["dt_bias"], lp["dt_bias_t"], lp["a_log"], lp["a_log_t"],
        lp["d_skip"], lp["ssd_norm_g"],
        consts["tril"], consts["triu"], consts["expand"],
    ]
    seq_map = lambda i, j: (i, j, 0)
    in_specs = [
        pl.BlockSpec((1, tile, d), seq_map),
        pl.BlockSpec((1, tile, LANES), seq_map),
        pl.BlockSpec((1, tile, LANES), seq_map),
        _const_spec(lp["g_mix"].shape),
        _layer_spec((d, proj_cols), (layer, 0, 0)),
    ] + [_const_spec(a.shape) for a in small]
    out_shape = [
        jax.ShapeDtypeStruct((b, s, CONV_CH + SSD_W), BF16),
        jax.ShapeDtypeStruct((b, s, DIFF_W), BF16),
        jax.ShapeDtypeStruct((b, s, DIFF_W), BF16),
        jax.ShapeDtypeStruct((b, s, DIFF_W), BF16),
    ]
    out_specs = [
        pl.BlockSpec((1, tile, CONV_CH + SSD_W), seq_map),
        pl.BlockSpec((1, tile, DIFF_W), seq_map),
        pl.BlockSpec((1, tile, DIFF_W), seq_map),
        pl.BlockSpec((1, tile, DIFF_W), seq_map),
    ]
    scratch = [
        pltpu.VMEM((tile, d), BF16),
        pltpu.VMEM((tile, 2 * CONV_CH), F32),
        pltpu.VMEM((tile, DIFF_W), F32),
        pltpu.VMEM((tile, DIFF_W), F32),
        pltpu.VMEM((tile, SSD_W), F32),
        pltpu.VMEM((tile, SSD_XBC), F32),
        pltpu.VMEM((tile, LANES), F32),
        pltpu.VMEM((tile // ROW_CHUNK, 2 * SUBLANES, ROW_CHUNK), F32),
        pltpu.VMEM((tile + CONV_HALO, CONV_CH), F32),
        pltpu.VMEM((tile + SMALL_HALO, SSD_XBC), F32),
        pltpu.VMEM((SSD_GROUPS, SSD_STATE, SSD_W // SSD_GROUPS), F32),
    ]
    return pl.pallas_call(
        functools.partial(_mix_in_kernel, tile=tile),
        grid=(b, s // tile),
        in_specs=in_specs,
        out_specs=out_specs,
        out_shape=out_shape,
        scratch_shapes=scratch,
        compiler_params=pltpu.CompilerParams(
            dimension_semantics=("parallel", "arbitrary"), vmem_limit_bytes=VMEM_LIMIT_BYTES),
        name="mix_in",
    )(x, cos_t, sin_t, lp["g_mix"], wts["w_in"], *small)


def _attn_kernel(lamv_ref, subg_ref, q_ref, k_ref, v_ref, o_ref, vext, q4_scr, acc_scr, m_scr, sa, sb,
                 *, blk, lam_init):
    i = pl.program_id(2)
    n_heads = LANES // DIFF_VD
    n_maps = 2 * n_heads
    rows = n_maps * blk
    lane = lax.broadcasted_iota(jnp.int32, (1, LANES), 1)
    lo_half = lane < DIFF_VD

    @pl.when(i == 0)
    def _():
        vb = v_ref[0]
        one = jnp.ones_like(vb)
        vext[0] = jnp.where(lo_half, vb, one)
        vext[1] = jnp.where(lo_half, one, vb)

    qb = q_ref[0]
    for j in range(n_maps):
        keep = (lane >= j * DIFF_HD) & (lane < (j + 1) * DIFF_HD)
        q4_scr[j * blk:(j + 1) * blk, :] = jnp.where(keep, qb, jnp.zeros_like(qb))
    m_scr[...] = jnp.full((rows, LANES), -jnp.inf, F32)
    acc_scr[...] = jnp.zeros((rows, LANES), F32)

    row_i = lax.broadcasted_iota(jnp.int32, (blk, blk), 0)
    col_i = lax.broadcasted_iota(jnp.int32, (blk, blk), 1)
    causal = col_i <= row_i

    def scores(kb, s_ref):
        k0 = pl.multiple_of(kb * blk, blk)
        s_ref[...] = _dot_nt(q4_scr[...], k_ref[0, pl.ds(k0, blk), :])

    def consume(kb, s_ref, masked):
        k0 = pl.multiple_of(kb * blk, blk)
        s = s_ref[...]
        if masked:
            s = jnp.concatenate(
                [jnp.where(causal, s[j * blk:(j + 1) * blk, :], -jnp.inf) for j in range(n_maps)], axis=0)
        cols = [s[:, c * LANES:(c + 1) * LANES] for c in range(blk // LANES)]
        rm = cols[0]
        for cc in cols[1:]:
            rm = jnp.maximum(rm, cc)
        m_old = m_scr[...]
        m_new = jnp.maximum(m_old, jnp.max(rm, axis=-1, keepdims=True))
        alpha = jnp.exp2(m_old - m_new)
        p = jnp.concatenate([jnp.exp2(cc - m_new) for cc in cols], axis=1).astype(BF16)
        hr = rows // n_heads
        pv = jnp.concatenate(
            [_dot(p[h * hr:(h + 1) * hr, :], vext[h, pl.ds(k0, blk), :]) for h in range(n_heads)], axis=0)
        acc_scr[...] = acc_scr[...] * alpha + pv
        m_scr[...] = m_new

    scores(0, sa)

    def pair(t, carry):
        scores(2 * t + 1, sb)
        consume(2 * t, sa, False)
        scores(2 * t + 2, sa)
        consume(2 * t + 1, sb, False)
        return carry

    lax.fori_loop(0, i // 2, pair, 0)

    @pl.when(i % 2 == 0)
    def _():
        consume(i, sa, True)

    @pl.when(i % 2 == 1)
    def _():
        scores(i, sb)
        consume(i - 1, sa, False)
        consume(i, sb, True)

    lv = lamv_ref[...]
    lam = (jnp.exp(jnp.sum(lv[0:1] * lv[1:2], axis=-1, keepdims=True))
           - jnp.exp(jnp.sum(lv[2:3] * lv[3:4], axis=-1, keepdims=True)) + lam_init)
    outs = []
    for h in range(n_heads):
        a1 = acc_scr[(2 * h) * blk:(2 * h + 1) * blk, :]
        a2 = acc_scr[(2 * h + 1) * blk:(2 * h + 2) * blk, :]
        r1 = a1 / pltpu.roll(a1, DIFF_VD, axis=1)
        r2 = a2 / pltpu.roll(a2, DIFF_VD, axis=1)
        outs.append(r1 - lam * r2)
    o = jnp.where(lo_half, outs[0], outs[1])
    o2 = o * o
    ms_lo = jnp.sum(jnp.where(lo_half, o2, 0.0), axis=-1, keepdims=True)
    ms_hi = jnp.sum(jnp.where(lo_half, 0.0, o2), axis=-1, keepdims=True)
    ms = jnp.where(lo_half, ms_lo, ms_hi) * (1.0 / DIFF_VD)
    o_ref[0] = (o * lax.rsqrt(ms + EPS) * subg_ref[...]).astype(BF16)


def _diff_attn(q, k, v, lamv, subg, lam_init, blk):
    b, s, w = q.shape
    n_pairs = w // LANES
    n_heads = LANES // DIFF_VD
    return pl.pallas_call(
        functools.partial(_attn_kernel, blk=blk, lam_init=lam_init),
        grid=(b, n_pairs, s // blk),
        in_specs=[
            pl.BlockSpec((4, LANES), lambda bi, pi, i: (0, 0)),
            pl.BlockSpec((1, LANES), lambda bi, pi, i: (0, 0)),
            pl.BlockSpec((1, blk, LANES), lambda bi, pi, i: (bi, i, pi)),
            pl.BlockSpec((1, s, LANES), lambda bi, pi, i: (bi, 0, pi)),
            pl.BlockSpec((1, s, LANES), lambda bi, pi, i: (bi, 0, pi)),
        ],
        out_specs=pl.BlockSpec((1, blk, LANES), lambda bi, pi, i: (bi, i, pi)),
        out_shape=jax.ShapeDtypeStruct((b, s, w), BF16),
        scratch_shapes=[
            pltpu.VMEM((n_heads, s, LANES), BF16),
            pltpu.VMEM((2 * n_heads * blk, LANES), BF16),
            pltpu.VMEM((2 * n_heads * blk, LANES), F32),
            pltpu.VMEM((2 * n_heads * blk, LANES), F32),
            pltpu.VMEM((2 * n_heads * blk, blk), F32),
            pltpu.VMEM((2 * n_heads * blk, blk), F32),
        ],
        compiler_params=pltpu.CompilerParams(
            dimension_semantics=("parallel", "parallel", "arbitrary"), vmem_limit_bytes=VMEM_LIMIT_BYTES),
        name="diff_attn",
    )(lamv, subg, q, k, v)


def _mix_out_kernel(
        x_ref, ycs_ref, yd_ref, p_ref,
        wo_c_ref, wo_d_ref, wo_s_ref, gffn_ref, wg_ref, wv_ref, fw_ref, fb_ref, wdown_ref,
        gple_ref, wpg_ref, wple_ref,
        o_ref,
        h_scr, gate_buf,
        *, tile):
    d_model = x_ref.shape[-1]
    d_ff = wdown_ref.shape[0]

    @pl.when(pl.program_id(1) == 0)
    def _():
        gate_buf[0:SMALL_HALO, :] = jnp.zeros((SMALL_HALO, d_ff), F32)

    def rms(v, g_ref):
        ms = jnp.sum(v * v, axis=-1, keepdims=True) * (1.0 / d_model)
        return (v * lax.rsqrt(ms + EPS) * g_ref[...]).astype(BF16)

    x1 = (x_ref[0] + _dot(ycs_ref[0, :, 0:CONV_CH], wo_c_ref[...]) + _dot(yd_ref[0], wo_d_ref[...])
          + _dot(ycs_ref[0, :, CONV_CH:], wo_s_ref[...]))
    o_ref[0] = x1
    h_scr[...] = rms(x1, gffn_ref)

    hval = h_scr[...]
    gate_buf[SMALL_HALO:SMALL_HALO + tile, :] = _dot(hval, wg_ref[...])
    val = _dot(hval, wv_ref[...])
    base = SMALL_HALO - (FFN_CONV - 1)
    gate = fb_ref[...]
    for k in range(FFN_CONV):
        gate = gate + fw_ref[k:k + 1, :] * gate_buf[base + k:base + k + tile, :]
    hid = (_silu(gate) * val).astype(BF16)
    o_ref[0] = o_ref[0] + _dot(hid, wdown_ref[...])
    gate_buf[0:SMALL_HALO, :] = gate_buf[tile:tile + SMALL_HALO, :]

    x2 = o_ref[0]
    gate = _sigmoid(_dot(rms(x2, gple_ref), wpg_ref[...]))
    o_ref[0] = x2 + _dot(p_ref[0].astype(BF16), wple_ref[...]) * gate


def _mix_out(x, ycs, ydiff, p, layer, wts, lp, tile):
    b, s, d = x.shape
    d_ff = wts["w_down"].shape[1]
    d_ple = p.shape[-1]
    seq_map = lambda i, j: (i, j, 0)
    in_specs = [
        pl.BlockSpec((1, tile, d), seq_map),
        pl.BlockSpec((1, tile, ycs.shape[-1]), seq_map),
        pl.BlockSpec((1, tile, ydiff.shape[-1]), seq_map),
        pl.BlockSpec((None, 1, tile, d_ple), lambda i, j: (layer, i, j, 0)),
        _layer_spec((CONV_CH, d), (layer, 0, 0)),
        _layer_spec((DIFF_W, d), (layer, CONV_CH // DIFF_W, 0)),
        _layer_spec((SSD_W, d), (layer, (CONV_CH + DIFF_W) // SSD_W, 0)),
        _const_spec(lp["g_ffn"].shape),
        _layer_spec((d, d_ff), (layer, 0, 0)),
        _layer_spec((d, d_ff), (layer, 0, 1)),
        _const_spec(lp["ffn_w"].shape), _const_spec(lp["ffn_b"].shape),
        _layer_spec((d_ff, d), (layer, 0, 0)),
        _const_spec(lp["g_ple"].shape),
        _layer_spec((d, d), (layer, 0, 0)),
        _layer_spec((d_ple, d), (layer, 0, 0)),
    ]
    return pl.pallas_call(
        functools.partial(_mix_out_kernel, tile=tile),
        grid=(b, s // tile),
        in_specs=in_specs,
        out_specs=pl.BlockSpec((1, tile, d), seq_map),
        out_shape=jax.ShapeDtypeStruct((b, s, d), F32),
        scratch_shapes=[
            pltpu.VMEM((tile, d), BF16),
            pltpu.VMEM((tile + SMALL_HALO, d_ff), F32),
        ],
        compiler_params=pltpu.CompilerParams(
            dimension_semantics=("parallel", "arbitrary"), vmem_limit_bytes=VMEM_LIMIT_BYTES),
        name="mix_out",
    )(x, ycs, ydiff, p, wts["w_out"], wts["w_out"], wts["w_out"], lp["g_ffn"], wts["w_up"], wts["w_up"],
      lp["ffn_w"], lp["ffn_b"], wts["w_down"], lp["g_ple"], wts["w_ple_gate"], wts["w_ple"])


def _pad_rows(a, rows):
    return jnp.pad(a, ((0, rows - a.shape[0]), (0, 0)))


def _pad_lanes(a, lanes):
    return jnp.pad(a, ((0, 0), (0, lanes - a.shape[1])))


def _constants():
    r = jnp.arange(ROW_CHUNK)
    tril = (r[None, :] <= r[:, None]).astype(BF16)
    lane = jnp.arange(LANES)
    col = jnp.arange(SSD_W)
    expand = (lane[:, None] == (col[None, :] // SSD_HD)).astype(BF16)
    c2 = jnp.arange(DIFF_W)
    g32 = ((c2[:, None] // DIFF_HD) == (c2[None, :] // DIFF_HD)).astype(F32) / DIFF_HD
    return {"tril": tril, "triu": tril.T, "expand": expand, "g32": g32.astype(BF16)}


def _layer_params(i, w):
    proj_cols = 2 * CONV_CH + 3 * DIFF_W + SSD_W + SSD_XBC
    w_dt = w["w_in"][i][:, proj_cols:]
    row = lambda v: v[i][None, :].astype(F32)
    n_rep = DIFF_W // DIFF_HD
    lam_init = 0.8 - 0.6 * math.exp(-0.3 * i)
    return {
        "g_mix": row(w["g_mix"]),
        "w_dt": _pad_lanes(w_dt, LANES).astype(BF16),
        "w_dtt": _pad_rows(w_dt.T, 2 * SUBLANES).astype(BF16),
        "conv_w": _pad_rows(w["conv_dw_w"][i], 32).astype(F32),
        "conv_b": row(w["conv_dw_b"]), "conv_ln_g": row(w["conv_ln_g"]), "conv_ln_b": row(w["conv_ln_b"]),
        "gq": jnp.tile(w["q_norm_g"][i].astype(F32), n_rep)[None, :] * (DIFF_HD ** -0.5 * LOG2_E),
        "gk": jnp.tile(w["k_norm_g"][i].astype(F32), n_rep)[None, :],
        "ssd_w": _pad_rows(w["ssd_conv_w"][i], SUBLANES).astype(F32),
        "ssd_b": row(w["ssd_conv_b"]),
        "dt_bias": _pad_lanes(row(w["ssd_dt_bias"]), LANES),
        "dt_bias_t": jnp.broadcast_to(_pad_rows(w["ssd_dt_bias"][i][:, None].astype(F32), 2 * SUBLANES),
                                      (2 * SUBLANES, ROW_CHUNK)),
        "a_log": _pad_lanes(row(w["ssd_a_log"]), LANES),
        "a_log_t": jnp.broadcast_to(_pad_rows(w["ssd_a_log"][i][:, None].astype(F32), 2 * SUBLANES),
                                    (2 * SUBLANES, ROW_CHUNK)),
        "d_skip": jnp.repeat(w["ssd_d"][i].astype(F32), SSD_HD)[None, :],
        "ssd_norm_g": row(w["ssd_norm_g"]),
        "lamv": _pad_lanes(jnp.stack([w["lam_q1"][i], w["lam_k1"][i], w["lam_q2"][i], w["lam_k2"][i]]).astype(F32),
                           LANES),
        "subg": jnp.tile(w["attn_subln_g"][i].astype(F32), LANES // DIFF_VD)[None, :] * (1.0 - lam_init),
        "lam_init": lam_init,
        "g_ffn": row(w["g_ffn"]),
        "ffn_w": _pad_rows(w["ffn_dw_w"][i], SUBLANES).astype(F32), "ffn_b": row(w["ffn_dw_b"]),
        "g_ple": row(w["g_ple"]),
    }


def kernel(x, p, positions, g_mix, w_in, conv_dw_w, conv_dw_b, conv_ln_g, conv_ln_b, q_norm_g, k_norm_g, lam_q1, lam_k1, lam_q2, lam_k2, attn_subln_g, ssd_conv_w, ssd_conv_b, ssd_dt_bias, ssd_a_log, ssd_d, ssd_norm_g, w_out, g_ffn, w_up, ffn_dw_w, ffn_dw_b, w_down, g_ple, w_ple_gate, w_ple):
    w = dict(g_mix=g_mix, w_in=w_in, conv_dw_w=conv_dw_w, conv_dw_b=conv_dw_b, conv_ln_g=conv_ln_g,
             conv_ln_b=conv_ln_b, q_norm_g=q_norm_g, k_norm_g=k_norm_g, lam_q1=lam_q1, lam_k1=lam_k1,
             lam_q2=lam_q2, lam_k2=lam_k2, attn_subln_g=attn_subln_g, ssd_conv_w=ssd_conv_w,
             ssd_conv_b=ssd_conv_b, ssd_dt_bias=ssd_dt_bias, ssd_a_log=ssd_a_log, ssd_d=ssd_d,
             ssd_norm_g=ssd_norm_g, g_ffn=g_ffn, ffn_dw_w=ffn_dw_w, ffn_dw_b=ffn_dw_b, g_ple=g_ple)
    wts = {"w_in": w_in.astype(BF16), "w_out": w_out.astype(BF16), "w_up": w_up.astype(BF16),
           "w_down": w_down.astype(BF16), "w_ple_gate": w_ple_gate.astype(BF16), "w_ple": w_ple.astype(BF16)}
    b, s, d = x.shape
    depth = w_in.shape[0]
    tile = min(s, 512)
    blk = min(s, 512)
    assert s % tile == 0 and tile % ROW_CHUNK == 0 and s % blk == 0 and blk % LANES == 0
    assert w_up.shape[-1] % (2 * LANES) == 0 and CONV_CH == DIFF_W and (CONV_CH + DIFF_W) == SSD_W
    consts = _constants()
    cos_t, sin_t = _rope_tables(positions)
    for i in range(depth):
        lp = _layer_params(i, w)
        ycs, q, k, v = _mix_in(x, cos_t, sin_t, i, wts, lp, consts, tile)
        ydiff = _diff_attn(q, k, v, lp["lamv"], lp["subg"], lp["lam_init"], blk)
        x = _mix_out(x, ycs, ydiff, p, i, wts, lp, tile)
    return x
```

```python
import functools
import math

import jax
import jax.numpy as jnp
from jax import lax
from jax.experimental import pallas as pl
from jax.experimental.pallas import tpu as pltpu

F32 = jnp.float32
BF16 = jnp.bfloat16

CONV_CH = 256
CONV_WIDTH = 31
DIFF_HEADS = 4
DIFF_HD = 32
DIFF_VD = 64
DIFF_W = 256
SSD_HEADS = 8
SSD_HD = 64
SSD_W = 512
SSD_GROUPS = 2
SSD_STATE = 128
SSD_CONV = 4
SSD_CHUNK = 128
SSD_XBC = 1024
FFN_CONV = 3
ROPE_THETA = 10000.0
EPS = 1e-6
LOG2_E = math.log2(math.e)

LANES = 128
SUBLANES = 8
VMEM_LIMIT_BYTES = 56 * 1024 * 1024

CONV_HALO = 32
SMALL_HALO = 8
ROW_CHUNK = SSD_CHUNK


def _dot(a, b):
    return jnp.dot(a, b, preferred_element_type=F32)


def _dot_nt(a, b):
    return lax.dot_general(a, b, (((1,), (1,)), ((), ())), preferred_element_type=F32)


def _split_bf16(x, terms):
    parts = []
    r = x
    for t in range(terms):
        p = r.astype(BF16)
        parts.append(p)
        if t + 1 < terms:
            r = r - p.astype(F32)
    return parts


def _dot_exact_lhs(x, m, terms):
    parts = _split_bf16(x, terms)
    acc = _dot(parts[0], m)
    for p in parts[1:]:
        acc = acc + _dot(p, m)
    return acc


def _sigmoid(x):
    return 0.5 * jnp.tanh(0.5 * x) + 0.5


def _silu(x):
    h = 0.5 * x
    return h * jnp.tanh(h) + h


def _softplus(x):
    return jnp.maximum(x, 0.0) + jnp.log1p(jnp.exp(-jnp.abs(x)))


def _shift_rows(w, r):
    if r == 0:
        return w
    return pltpu.roll(w, w.shape[0] - r, axis=0)


def _causal_dwconv_rows(window, halo, width, w_ref, bias, rows):
    base = halo - (width - 1)
    acc = None
    for r in range(SUBLANES):
        taps = [o for o in range(base, base + width) if o % SUBLANES == r]
        if not taps:
            continue
        shifted = _shift_rows(window, r)
        for o in taps:
            k = o - base
            a = o - r
            term = w_ref[k:k + 1, :] * shifted[a:a + rows, :]
            acc = term if acc is None else acc + term
    return acc + bias


def _rope_kernel(pos_ref, invf_ref, sgn_ref, cos_ref, sin_ref):
    ang = pos_ref[0].astype(F32) * invf_ref[...]
    cos_ref[0] = jnp.cos(ang)
    sin_ref[0] = jnp.sin(ang) * sgn_ref[...]


def _rope_tables(positions):
    b, s = positions.shape
    t = min(s, 512)
    assert s % t == 0
    half = DIFF_HD // 2
    inv_freq = ROPE_THETA ** (-jnp.arange(0, DIFF_HD, 2, dtype=F32) / DIFF_HD)
    lane = jnp.arange(LANES)
    invf = inv_freq[(lane % DIFF_HD) % half][None, :]
    sgn = jnp.where((lane % DIFF_HD) < half, -1.0, 1.0).astype(F32)[None, :]
    return pl.pallas_call(
        _rope_kernel,
        grid=(b, s // t),
        in_specs=[
            pl.BlockSpec((1, t, 1), lambda i, j: (i, j, 0)),
            pl.BlockSpec((1, LANES), lambda i, j: (0, 0)),
            pl.BlockSpec((1, LANES), lambda i, j: (0, 0)),
        ],
        out_specs=[
            pl.BlockSpec((1, t, LANES), lambda i, j: (i, j, 0)),
            pl.BlockSpec((1, t, LANES), lambda i, j: (i, j, 0)),
        ],
        out_shape=[jax.ShapeDtypeStruct((b, s, LANES), F32)] * 2,
        compiler_params=pltpu.CompilerParams(dimension_semantics=("parallel", "parallel")),
        name="rope_tables",
    )(positions.reshape(b, s, 1), invf, sgn)


def _mix_in_kernel(
        x_ref, cos_ref, sin_ref, gmix_ref,
        win_ref, wdtt_ref,
        cw_ref, cb_ref, lng_ref, lnb_ref,
        gq_ref, gk_ref, g32_ref,
        sw_ref, sb_ref, dtbt_ref, alogt_ref, dskip_ref, ng_ref,
        triu_ref, exp_ref,
        ycs_ref, q_ref, k_ref, v_ref,
        hn_scr, uc_scr, q_scr, k_scr, z_scr, xbc_scr, dtt_scr, hbuf, xbuf, h_scr,
        *, tile):
    n_chunks = tile // ROW_CHUNK
    d_model = x_ref.shape[-1]
    edges = [0]
    for width in (2 * CONV_CH, DIFF_W, DIFF_W, DIFF_W, SSD_W, SSD_XBC):
        edges.append(edges[-1] + width)
    col = [slice(edges[j], edges[j + 1]) for j in range(6)]

    @pl.when(pl.program_id(1) == 0)
    def _():
        hbuf[0:CONV_HALO, :] = jnp.zeros((CONV_HALO, CONV_CH), F32)
        xbuf[0:SMALL_HALO, :] = jnp.zeros((SMALL_HALO, SSD_XBC), F32)
        h_scr[...] = jnp.zeros(h_scr.shape, F32)

    xr = x_ref[0]
    ms = jnp.sum(xr * xr, axis=-1, keepdims=True) * (1.0 / d_model)
    hn_scr[...] = (xr * lax.rsqrt(ms + EPS) * gmix_ref[...]).astype(BF16)

    hn = hn_scr[...]
    uc_scr[...] = _dot(hn, win_ref[:, col[0]])
    q_scr[...] = _dot(hn, win_ref[:, col[1]])
    k_scr[...] = _dot(hn, win_ref[:, col[2]])
    v_ref[0] = _dot(hn, win_ref[:, col[3]]).astype(BF16)
    z_scr[...] = _dot(hn, win_ref[:, col[4]])
    xbc_scr[...] = _dot(hn, win_ref[:, col[5]])
    dtt = _dot_nt(wdtt_ref[...], hn)
    for c in range(n_chunks):
        dtt_scr[c] = dtt[:, c * ROW_CHUNK:(c + 1) * ROW_CHUNK]

    lane128 = lax.broadcasted_iota(jnp.int32, (1, LANES), 1)
    lane256 = lax.broadcasted_iota(jnp.int32, (1, 2 * LANES), 1)
    first_half = (lane256 % DIFF_HD) < (DIFF_HD // 2)
    row_i = lax.broadcasted_iota(jnp.int32, (ROW_CHUNK, ROW_CHUNK), 0)
    col_i = lax.broadcasted_iota(jnp.int32, (ROW_CHUNK, ROW_CHUNK), 1)
    causal = col_i <= row_i
    a_col = -jnp.exp(alogt_ref[...])

    def norm_rope(t, g_ref, cos, sin):
        ms = _dot((t * t).astype(BF16), g32_ref[...])
        tn = t * lax.rsqrt(ms + EPS) * g_ref[...]
        fwd = pltpu.roll(tn, 2 * LANES - DIFF_HD // 2, axis=1)
        bwd = pltpu.roll(tn, DIFF_HD // 2, axis=1)
        rot = jnp.where(first_half, fwd, bwd)
        lo = tn[:, :LANES] * cos + rot[:, :LANES] * sin
        hi = tn[:, LANES:] * cos + rot[:, LANES:] * sin
        return jnp.concatenate([lo, hi], axis=1).astype(BF16)

    def mix_rows(c):
        r0 = c * ROW_CHUNK
        rows = pl.ds(r0, ROW_CHUNK)

        uc = uc_scr[rows, :]
        hbuf[pl.ds(r0 + CONV_HALO, ROW_CHUNK), :] = uc[:, :CONV_CH] * _sigmoid(uc[:, CONV_CH:])
        win = hbuf[pl.ds(r0, ROW_CHUNK + CONV_HALO), :]
        hc = _causal_dwconv_rows(win, CONV_HALO, CONV_WIDTH, cw_ref, cb_ref[...], ROW_CHUNK)
        mu = jnp.mean(hc, axis=-1, keepdims=True)
        xc = hc - mu
        var = jnp.mean(xc * xc, axis=-1, keepdims=True)
        yln = xc * lax.rsqrt(var + EPS) * lng_ref[...] + lnb_ref[...]
        ycs_ref[0, rows, 0:CONV_CH] = _silu(yln).astype(BF16)

        cos = cos_ref[0, rows, :]
        sin = sin_ref[0, rows, :]
        q_ref[0, rows, :] = norm_rope(q_scr[rows, :], gq_ref, cos, sin)
        k_ref[0, rows, :] = norm_rope(k_scr[rows, :], gk_ref, cos, sin)

        xbuf[pl.ds(r0 + SMALL_HALO, ROW_CHUNK), :] = xbc_scr[rows, :]
        xwin = xbuf[pl.ds(r0, ROW_CHUNK + SMALL_HALO), :]
        xbc = _silu(_causal_dwconv_rows(xwin, SMALL_HALO, SSD_CONV, sw_ref, sb_ref[...], ROW_CHUNK))
        xs = xbc[:, :SSD_W]
        bmat = xbc[:, SSD_W:SSD_W + SSD_GROUPS * SSD_STATE]
        cmat = xbc[:, SSD_W + SSD_GROUPS * SSD_STATE:]

        dt_t = _softplus(dtt_scr[c] + dtbt_ref[...])
        cs_t = _dot_exact_lhs(dt_t * a_col, triu_ref[...], 3)
        packed = jnp.concatenate(
            [dt_t[0:SSD_HEADS], cs_t[0:SSD_HEADS],
             jnp.zeros((ROW_CHUNK - 2 * SSD_HEADS, ROW_CHUNK), F32)], axis=0)
        cols_t = packed.T
        cs = pltpu.roll(cols_t, LANES - SSD_HEADS, axis=1)

        dt_w = _dot_exact_lhs(cols_t, exp_ref[...], 2)
        cs_w = _dot_exact_lhs(cs, exp_ref[...], 2)
        cs_last = cs_w[ROW_CHUNK - 1:ROW_CHUNK, :]
        xdt = xs * dt_w
        xdt_b = xdt.astype(BF16)
        xdec_b = (xdt * jnp.exp(cs_last - cs_w)).astype(BF16)
        ecs = jnp.exp(cs_w)
        chunk_decay = jnp.exp(cs_last)
        z = z_scr[rows, :]

        gw = SSD_W // SSD_GROUPS
        for g in range(SSD_GROUPS):
            bg = bmat[:, g * SSD_STATE:(g + 1) * SSD_STATE]
            cg_b = cmat[:, g * SSD_STATE:(g + 1) * SSD_STATE].astype(BF16)
            cb = _dot_nt(cg_b, bg.astype(BF16))
            yd_parts = []
            for pr in range(gw // LANES):
                xpair = xdt_b[:, g * gw + pr * LANES:g * gw + (pr + 1) * LANES]
                outs = []
                for hh in range(LANES // SSD_HD):
                    h = (g * gw + pr * LANES) // SSD_HD + hh
                    seg = cs[:, h:h + 1] - cs_t[h:h + 1, :]
                    lmat = jnp.exp(jnp.where(causal, seg, -jnp.inf))
                    outs.append(_dot((cb * lmat).astype(BF16), xpair))
                yd_parts.append(jnp.where(lane128 < SSD_HD, outs[0], outs[1]))
            yd = jnp.concatenate(yd_parts, axis=1)
            hstate = h_scr[g]
            yo = _dot(cg_b, hstate.astype(BF16)) * ecs[:, g * gw:(g + 1) * gw]
            st = _dot(bg.T.astype(BF16), xdec_b[:, g * gw:(g + 1) * gw])
            h_scr[g] = hstate * chunk_decay[:, g * gw:(g + 1) * gw] + st
            yg = (yd + yo + xs[:, g * gw:(g + 1) * gw] * dskip_ref[:, g * gw:(g + 1) * gw])
            yg = yg * _silu(z[:, g * gw:(g + 1) * gw])
            ms = jnp.mean(yg * yg, axis=-1, keepdims=True)
            yn = yg * lax.rsqrt(ms + EPS) * ng_ref[:, g * gw:(g + 1) * gw]
            ycs_ref[0, rows, CONV_CH + g * gw:CONV_CH + (g + 1) * gw] = yn.astype(BF16)

    for c in range(n_chunks):
        mix_rows(c)

    hbuf[0:CONV_HALO, :] = hbuf[tile:tile + CONV_HALO, :]
    xbuf[0:SMALL_HALO, :] = xbuf[tile:tile + SMALL_HALO, :]


def _const_spec(shape):
    nd = len(shape)
    return pl.BlockSpec(shape, lambda *_: (0,) * nd, pipeline_mode=pl.Buffered(1))


def _layer_spec(block, index):
    return pl.BlockSpec((None,) + tuple(block), lambda *_: tuple(index), pipeline_mode=pl.Buffered(1))


def _mix_in(x, cos_t, sin_t, layer, wts, lp, consts, tile):
    b, s, d = x.shape
    proj_cols = 2 * CONV_CH + 3 * DIFF_W + SSD_W + SSD_XBC
    small = [
        lp["w_dtt"],
        lp["conv_w"], lp["conv_b"], lp["conv_ln_g"], lp["conv_ln_b"],
        lp["gq"], lp["gk"], consts["g32"],
        lp["ssd_w"], lp["ssd_b"], lp["dt_bias_t"], lp["a_log_t"],
        lp["d_skip"], lp["ssd_norm_g"],
        consts["triu"], consts["expand"],
    ]
    seq_map = lambda i, j: (i, j, 0)
    in_specs = [
        pl.BlockSpec((1, tile, d), seq_map),
        pl.BlockSpec((1, tile, LANES), seq_map),
        pl.BlockSpec((1, tile, LANES), seq_map),
        _const_spec(lp["g_mix"].shape),
        _layer_spec((d, proj_cols), (layer, 0, 0)),
    ] + [_const_spec(a.shape) for a in small]
    out_shape = [
        jax.ShapeDtypeStruct((b, s, CONV_CH + SSD_W), BF16),
        jax.ShapeDtypeStruct((b, s, DIFF_W), BF16),
        jax.ShapeDtypeStruct((b, s, DIFF_W), BF16),
        jax.ShapeDtypeStruct((b, s, DIFF_W), BF16),
    ]
    out_specs = [
        pl.BlockSpec((1, tile, CONV_CH + SSD_W), seq_map),
        pl.BlockSpec((1, tile, DIFF_W), seq_map),
        pl.BlockSpec((1, tile, DIFF_W), seq_map),
        pl.BlockSpec((1, tile, DIFF_W), seq_map),
    ]
    scratch = [
        pltpu.VMEM((tile, d), BF16),
        pltpu.VMEM((tile, 2 * CONV_CH), F32),
        pltpu.VMEM((tile, DIFF_W), F32),
        pltpu.VMEM((tile, DIFF_W), F32),
        pltpu.VMEM((tile, SSD_W), F32),
        pltpu.VMEM((tile, SSD_XBC), F32),
        pltpu.VMEM((tile // ROW_CHUNK, 2 * SUBLANES, ROW_CHUNK), F32),
        pltpu.VMEM((tile + CONV_HALO, CONV_CH), F32),
        pltpu.VMEM((tile + SMALL_HALO, SSD_XBC), F32),
        pltpu.VMEM((SSD_GROUPS, SSD_STATE, SSD_W // SSD_GROUPS), F32),
    ]
    return pl.pallas_call(
        functools.partial(_mix_in_kernel, tile=tile),
        grid=(b, s // tile),
        in_specs=in_specs,
        out_specs=out_specs,
        out_shape=out_shape,
        scratch_shapes=scratch,
        compiler_params=pltpu.CompilerParams(
            dimension_semantics=("parallel", "arbitrary"), vmem_limit_bytes=VMEM_LIMIT_BYTES),
        name="mix_in",
    )(x, cos_t, sin_t, lp["g_mix"], wts["w_in"], *small)


def _attn_kernel(lamv_ref, subg_ref, q_ref, k_ref, v_ref, o_ref, vext, q4_scr, acc_scr, m_scr, sa, sb,
                 *, blk, lam_init):
    i = pl.program_id(2)
    n_heads = LANES // DIFF_VD
    n_maps = 2 * n_heads
    rows = n_maps * blk
    lane = lax.broadcasted_iota(jnp.int32, (1, LANES), 1)
    lo_half = lane < DIFF_VD

    @pl.when(i == 0)
    def _():
        vb = v_ref[0]
        one = jnp.ones_like(vb)
        vext[0] = jnp.where(lo_half, vb, one)
        vext[1] = jnp.where(lo_half, one, vb)

    qb = q_ref[0]
    for j in range(n_maps):
        keep = (lane >= j * DIFF_HD) & (lane < (j + 1) * DIFF_HD)
        q4_scr[j * blk:(j + 1) * blk, :] = jnp.where(keep, qb, jnp.zeros_like(qb))
    m_scr[...] = jnp.full((rows, LANES), -jnp.inf, F32)
    acc_scr[...] = jnp.zeros((rows, LANES), F32)

    row_i = lax.broadcasted_iota(jnp.int32, (blk, blk), 0)
    col_i = lax.broadcasted_iota(jnp.int32, (blk, blk), 1)
    causal = col_i <= row_i

    def scores(kb, s_ref):
        k0 = pl.multiple_of(kb * blk, blk)
        s_ref[...] = _dot_nt(q4_scr[...], k_ref[0, pl.ds(k0, blk), :])

    def consume(kb, s_ref, masked):
        k0 = pl.multiple_of(kb * blk, blk)
        s = s_ref[...]
        if masked:
            s = jnp.concatenate(
                [jnp.where(causal, s[j * blk:(j + 1) * blk, :], -jnp.inf) for j in range(n_maps)], axis=0)
        cols = [s[:, c * LANES:(c + 1) * LANES] for c in range(blk // LANES)]
        rm = cols[0]
        for cc in cols[1:]:
            rm = jnp.maximum(rm, cc)
        m_old = m_scr[...]
        m_new = jnp.maximum(m_old, jnp.max(rm, axis=-1, keepdims=True))
        alpha = jnp.exp2(m_old - m_new)
        p = jnp.concatenate([jnp.exp2(cc - m_new) for cc in cols], axis=1).astype(BF16)
        hr = rows // n_heads
        pv = jnp.concatenate(
            [_dot(p[h * hr:(h + 1) * hr, :], vext[h, pl.ds(k0, blk), :]) for h in range(n_heads)], axis=0)
        acc_scr[...] = acc_scr[...] * alpha + pv
        m_scr[...] = m_new

    scores(0, sa)

    def pair(t, carry):
        scores(2 * t + 1, sb)
        consume(2 * t, sa, False)
        scores(2 * t + 2, sa)
        consume(2 * t + 1, sb, False)
        return carry

    lax.fori_loop(0, i // 2, pair, 0)

    @pl.when(i % 2 == 0)
    def _():
        consume(i, sa, True)

    @pl.when(i % 2 == 1)
    def _():
        scores(i, sb)
        consume(i - 1, sa, False)
        consume(i, sb, True)

    lv = lamv_ref[...]
    lam = (jnp.exp(jnp.sum(lv[0:1] * lv[1:2], axis=-1, keepdims=True))
           - jnp.exp(jnp.sum(lv[2:3] * lv[3:4], axis=-1, keepdims=True)) + lam_init)
    outs = []
    for h in range(n_heads):
        a1 = acc_scr[(2 * h) * blk:(2 * h + 1) * blk, :]
        a2 = acc_scr[(2 * h + 1) * blk:(2 * h + 2) * blk, :]
        r1 = a1 / pltpu.roll(a1, DIFF_VD, axis=1)
        r2 = a2 / pltpu.roll(a2, DIFF_VD, axis=1)
        outs.append(r1 - lam * r2)
    o = jnp.where(lo_half, outs[0], outs[1])
    o2 = o * o
    ms_lo = jnp.sum(jnp.where(lo_half, o2, 0.0), axis=-1, keepdims=True)
    ms_hi = jnp.sum(jnp.where(lo_half, 0.0, o2), axis=-1, keepdims=True)
    ms = jnp.where(lo_half, ms_lo, ms_hi) * (1.0 / DIFF_VD)
    o_ref[0] = (o * lax.rsqrt(ms + EPS) * subg_ref[...]).astype(BF16)


def _diff_attn(q, k, v, lamv, subg, lam_init, blk):
    b, s, w = q.shape
    n_pairs = w // LANES
    n_heads = LANES // DIFF_VD
    return pl.pallas_call(
        functools.partial(_attn_kernel, blk=blk, lam_init=lam_init),
        grid=(b, n_pairs, s // blk),
        in_specs=[
            pl.BlockSpec((4, LANES), lambda bi, pi, i: (0, 0)),
            pl.BlockSpec((1, LANES), lambda bi, pi, i: (0, 0)),
            pl.BlockSpec((1, blk, LANES), lambda bi, pi, i: (bi, i, pi)),
            pl.BlockSpec((1, s, LANES), lambda bi, pi, i: (bi, 0, pi)),
            pl.BlockSpec((1, s, LANES), lambda bi, pi, i: (bi, 0, pi)),
        ],
        out_specs=pl.BlockSpec((1, blk, LANES), lambda bi, pi, i: (bi, i, pi)),
        out_shape=jax.ShapeDtypeStruct((b, s, w), BF16),
        scratch_shapes=[
            pltpu.VMEM((n_heads, s, LANES), BF16),
            pltpu.VMEM((2 * n_heads * blk, LANES), BF16),
            pltpu.VMEM((2 * n_heads * blk, LANES), F32),
            pltpu.VMEM((2 * n_heads * blk, LANES), F32),
            pltpu.VMEM((2 * n_heads * blk, blk), F32),
            pltpu.VMEM((2 * n_heads * blk, blk), F32),
        ],
        compiler_params=pltpu.CompilerParams(
            dimension_semantics=("parallel", "parallel", "arbitrary"), vmem_limit_bytes=VMEM_LIMIT_BYTES),
        name="diff_attn",
    )(lamv, subg, q, k, v)


def _mix_out_kernel(
        x_ref, ycs_ref, yd_ref, p_ref,
        wo_c_ref, wo_d_ref, wo_s_ref, gffn_ref, wg_ref, wv_ref, fw_ref, fb_ref, wdown_ref,
        gple_ref, wpg_ref, wple_ref,
        o_ref,
        h_scr, gate_buf,
        *, tile):
    d_model = x_ref.shape[-1]
    d_ff = wdown_ref.shape[0]

    @pl.when(pl.program_id(1) == 0)
    def _():
        gate_buf[0:SMALL_HALO, :] = jnp.zeros((SMALL_HALO, d_ff), F32)

    def rms(v, g_ref):
        ms = jnp.sum(v * v, axis=-1, keepdims=True) * (1.0 / d_model)
        return (v * lax.rsqrt(ms + EPS) * g_ref[...]).astype(BF16)

    x1 = (x_ref[0] + _dot(ycs_ref[0, :, 0:CONV_CH], wo_c_ref[...]) + _dot(yd_ref[0], wo_d_ref[...])
          + _dot(ycs_ref[0, :, CONV_CH:], wo_s_ref[...]))
    o_ref[0] = x1
    h_scr[...] = rms(x1, gffn_ref)

    hval = h_scr[...]
    gate_buf[SMALL_HALO:SMALL_HALO + tile, :] = _dot(hval, wg_ref[...])
    val = _dot(hval, wv_ref[...])
    base = SMALL_HALO - (FFN_CONV - 1)
    gate = fb_ref[...]
    for k in range(FFN_CONV):
        gate = gate + fw_ref[k:k + 1, :] * gate_buf[base + k:base + k + tile, :]
    hid = (_silu(gate) * val).astype(BF16)
    o_ref[0] = o_ref[0] + _dot(hid, wdown_ref[...])
    gate_buf[0:SMALL_HALO, :] = gate_buf[tile:tile + SMALL_HALO, :]

    x2 = o_ref[0]
    gate = _sigmoid(_dot(rms(x2, gple_ref), wpg_ref[...]))
    o_ref[0] = x2 + _dot(p_ref[0].astype(BF16), wple_ref[...]) * gate


def _mix_out(x, ycs, ydiff, p, layer, wts, lp, tile):
    b, s, d = x.shape
    d_ff = wts["w_down"].shape[1]
    d_ple = p.shape[-1]
    seq_map = lambda i, j: (i, j, 0)
    in_specs = [
        pl.BlockSpec((1, tile, d), seq_map),
        pl.BlockSpec((1, tile, ycs.shape[-1]), seq_map),
        pl.BlockSpec((1, tile, ydiff.shape[-1]), seq_map),
        pl.BlockSpec((None, 1, tile, d_ple), lambda i, j: (layer, i, j, 0)),
        _layer_spec((CONV_CH, d), (layer, 0, 0)),
        _layer_spec((DIFF_W, d), (layer, CONV_CH // DIFF_W, 0)),
        _layer_spec((SSD_W, d), (layer, (CONV_CH + DIFF_W) // SSD_W, 0)),
        _const_spec(lp["g_ffn"].shape),
        _layer_spec((d, d_ff), (layer, 0, 0)),
        _layer_spec((d, d_ff), (layer, 0, 1)),
        _const_spec(lp["ffn_w"].shape), _const_spec(lp["ffn_b"].shape),
        _layer_spec((d_ff, d), (layer, 0, 0)),
        _const_spec(lp["g_ple"].shape),
        _layer_spec((d, d), (layer, 0, 0)),
        _layer_spec((d_ple, d), (layer, 0, 0)),
    ]
    return pl.pallas_call(
        functools.partial(_mix_out_kernel, tile=tile),
        grid=(b, s // tile),
        in_specs=in_specs,
        out_specs=pl.BlockSpec((1, tile, d), seq_map),
        out_shape=jax.ShapeDtypeStruct((b, s, d), F32),
        scratch_shapes=[
            pltpu.VMEM((tile, d), BF16),
            pltpu.VMEM((tile + SMALL_HALO, d_ff), F32),
        ],
        compiler_params=pltpu.CompilerParams(
            dimension_semantics=("parallel", "arbitrary"), vmem_limit_bytes=VMEM_LIMIT_BYTES),
        name="mix_out",
    )(x, ycs, ydiff, p, wts["w_out"], wts["w_out"], wts["w_out"], lp["g_ffn"], wts["w_up"], wts["w_up"],
      lp["ffn_w"], lp["ffn_b"], wts["w_down"], lp["g_ple"], wts["w_ple_gate"], wts["w_ple"])


def _pad_rows(a, rows):
    return jnp.pad(a, ((0, rows - a.shape[0]), (0, 0)))


def _pad_lanes(a, lanes):
    return jnp.pad(a, ((0, 0), (0, lanes - a.shape[1])))


def _constants():
    r = jnp.arange(ROW_CHUNK)
    triu = (r[:, None] <= r[None, :]).astype(BF16)
    lane = jnp.arange(LANES)
    col = jnp.arange(SSD_W)
    expand = (lane[:, None] == (col[None, :] // SSD_HD)).astype(BF16)
    c2 = jnp.arange(DIFF_W)
    g32 = ((c2[:, None] // DIFF_HD) == (c2[None, :] // DIFF_HD)).astype(F32) / DIFF_HD
    return {"triu": triu, "expand": expand, "g32": g32.astype(BF16)}


def _layer_params(i, w):
    proj_cols = 2 * CONV_CH + 3 * DIFF_W + SSD_W + SSD_XBC
    w_dt = w["w_in"][i][:, proj_cols:]
    row = lambda v: v[i][None, :].astype(F32)
    n_rep = DIFF_W // DIFF_HD
    lam_init = 0.8 - 0.6 * math.exp(-0.3 * i)
    return {
        "g_mix": row(w["g_mix"]),
        "w_dtt": _pad_rows(w_dt.T, 2 * SUBLANES).astype(BF16),
        "conv_w": _pad_rows(w["conv_dw_w"][i], 32).astype(F32),
        "conv_b": row(w["conv_dw_b"]), "conv_ln_g": row(w["conv_ln_g"]), "conv_ln_b": row(w["conv_ln_b"]),
        "gq": jnp.tile(w["q_norm_g"][i].astype(F32), n_rep)[None, :] * (DIFF_HD ** -0.5 * LOG2_E),
        "gk": jnp.tile(w["k_norm_g"][i].astype(F32), n_rep)[None, :],
        "ssd_w": _pad_rows(w["ssd_conv_w"][i], SUBLANES).astype(F32),
        "ssd_b": row(w["ssd_conv_b"]),
        "dt_bias_t": jnp.broadcast_to(_pad_rows(w["ssd_dt_bias"][i][:, None].astype(F32), 2 * SUBLANES),
                                      (2 * SUBLANES, ROW_CHUNK)),
        "a_log_t": jnp.broadcast_to(_pad_rows(w["ssd_a_log"][i][:, None].astype(F32), 2 * SUBLANES),
                                    (2 * SUBLANES, ROW_CHUNK)),
        "d_skip": jnp.repeat(w["ssd_d"][i].astype(F32), SSD_HD)[None, :],
        "ssd_norm_g": row(w["ssd_norm_g"]),
        "lamv": _pad_lanes(jnp.stack([w["lam_q1"][i], w["lam_k1"][i], w["lam_q2"][i], w["lam_k2"][i]]).astype(F32),
                           LANES),
        "subg": jnp.tile(w["attn_subln_g"][i].astype(F32), LANES // DIFF_VD)[None, :] * (1.0 - lam_init),
        "lam_init": lam_init,
        "g_ffn": row(w["g_ffn"]),
        "ffn_w": _pad_rows(w["ffn_dw_w"][i], SUBLANES).astype(F32), "ffn_b": row(w["ffn_dw_b"]),
        "g_ple": row(w["g_ple"]),
    }


def kernel(x, p, positions, g_mix, w_in, conv_dw_w, conv_dw_b, conv_ln_g, conv_ln_b, q_norm_g, k_norm_g, lam_q1, lam_k1, lam_q2, lam_k2, attn_subln_g, ssd_conv_w, ssd_conv_b, ssd_dt_bias, ssd_a_log, ssd_d, ssd_norm_g, w_out, g_ffn, w_up, ffn_dw_w, ffn_dw_b, w_down, g_ple, w_ple_gate, w_ple):
    w = dict(g_mix=g_mix, w_in=w_in, conv_dw_w=conv_dw_w, conv_dw_b=conv_dw_b, conv_ln_g=conv_ln_g,
             conv_ln_b=conv_ln_b, q_norm_g=q_norm_g, k_norm_g=k_norm_g, lam_q1=lam_q1, lam_k1=lam_k1,
             lam_q2=lam_q2, lam_k2=lam_k2, attn_subln_g=attn_subln_g, ssd_conv_w=ssd_conv_w,
             ssd_conv_b=ssd_conv_b, ssd_dt_bias=ssd_dt_bias, ssd_a_log=ssd_a_log, ssd_d=ssd_d,
             ssd_norm_g=ssd_norm_g, g_ffn=g_ffn, ffn_dw_w=ffn_dw_w, ffn_dw_b=ffn_dw_b, g_ple=g_ple)
    wts = {"w_in": w_in.astype(BF16), "w_out": w_out.astype(BF16), "w_up": w_up.astype(BF16),
           "w_down": w_down.astype(BF16), "w_ple_gate": w_ple_gate.astype(BF16), "w_ple": w_ple.astype(BF16)}
    b, s, d = x.shape
    depth = w_in.shape[0]
    tile = min(s, 512)
    blk = min(s, 512)
    assert s % tile == 0 and tile % ROW_CHUNK == 0 and s % blk == 0 and blk % LANES == 0
    assert w_up.shape[-1] % (2 * LANES) == 0 and CONV_CH == DIFF_W and (CONV_CH + DIFF_W) == SSD_W
    consts = _constants()
    cos_t, sin_t = _rope_tables(positions)
    for i in range(depth):
        lp = _layer_params(i, w)
        ycs, q, k, v = _mix_in(x, cos_t, sin_t, i, wts, lp, consts, tile)
        ydiff = _diff_attn(q, k, v, lp["lamv"], lp["subg"], lp["lam_init"], blk)
        x = _mix_out(x, ycs, ydiff, p, i, wts, lp, tile)
    return x
```

```python
import functools
import math

import jax
import jax.numpy as jnp
from jax import lax
from jax.experimental import pallas as pl
from jax.experimental.pallas import tpu as pltpu

F32 = jnp.float32
BF16 = jnp.bfloat16

CONV_CH = 256
CONV_WIDTH = 31
DIFF_HEADS = 4
DIFF_HD = 32
DIFF_VD = 64
DIFF_W = 256
SSD_HEADS = 8
SSD_HD = 64
SSD_W = 512
SSD_GROUPS = 2
SSD_STATE = 128
SSD_CONV = 4
SSD_CHUNK = 128
SSD_XBC = 1024
FFN_CONV = 3
ROPE_THETA = 10000.0
EPS = 1e-6
LOG2_E = math.log2(math.e)

LANES = 128
SUBLANES = 8
VMEM_LIMIT_BYTES = 56 * 1024 * 1024

CONV_HALO = 32
SMALL_HALO = 8
ROW_CHUNK = SSD_CHUNK


def _dot(a, b):
    return jnp.dot(a, b, preferred_element_type=F32)


def _dot_nt(a, b):
    return lax.dot_general(a, b, (((1,), (1,)), ((), ())), preferred_element_type=F32)


def _split_bf16(x, terms):
    parts = []
    r = x
    for t in range(terms):
        p = r.astype(BF16)
        parts.append(p)
        if t + 1 < terms:
            r = r - p.astype(F32)
    return parts


def _dot_exact_lhs(x, m, terms):
    parts = _split_bf16(x, terms)
    acc = _dot(parts[0], m)
    for p in parts[1:]:
        acc = acc + _dot(p, m)
    return acc


def _sigmoid(x):
    return 0.5 * jnp.tanh(0.5 * x) + 0.5


def _silu(x):
    h = 0.5 * x
    return h * jnp.tanh(h) + h


def _softplus(x):
    return jnp.maximum(x, 0.0) + jnp.log1p(jnp.exp(-jnp.abs(x)))


def _shift_rows(w, r):
    if r == 0:
        return w
    return pltpu.roll(w, w.shape[0] - r, axis=0)


def _causal_dwconv_rows(window, halo, width, w_ref, bias, rows):
    base = halo - (width - 1)
    acc = None
    for r in range(SUBLANES):
        taps = [o for o in range(base, base + width) if o % SUBLANES == r]
        if not taps:
            continue
        shifted = _shift_rows(window, r)
        for o in taps:
            k = o - base
            a = o - r
            term = w_ref[k:k + 1, :] * shifted[a:a + rows, :]
            acc = term if acc is None else acc + term
    return acc + bias


ROPE_FOLD = LANES // (DIFF_HD // 2)


def _rope_kernel(pos_ref, invf_ref, selc_ref, sels_ref, cos_ref, sin_ref):
    ang = pos_ref[0].astype(F32) * invf_ref[...]
    rows = ang.shape[0]
    for table, sel_ref, out_ref in ((jnp.cos(ang), selc_ref, cos_ref), (jnp.sin(ang), sels_ref, sin_ref)):
        wide = _dot_exact_lhs(table, sel_ref[...], 3)
        for slot in range(ROPE_FOLD):
            out_ref[0, slot * rows:(slot + 1) * rows, :] = wide[:, slot * LANES:(slot + 1) * LANES]


def _rope_tables(positions):
    b, s = positions.shape
    half = DIFF_HD // 2
    rows = s // ROPE_FOLD
    assert s % (ROPE_FOLD * SUBLANES) == 0
    inv_freq = ROPE_THETA ** (-jnp.arange(0, DIFF_HD, 2, dtype=F32) / DIFF_HD)
    lane = jnp.arange(LANES)
    invf = inv_freq[lane % half][None, :]
    pos_c = jnp.repeat(positions.reshape(b, ROPE_FOLD, rows).transpose(0, 2, 1), half, axis=-1)
    out_lane = jnp.arange(ROPE_FOLD * LANES)
    src = (out_lane // LANES) * half + (out_lane % DIFF_HD) % half
    sel = (lane[:, None] == src[None, :])
    sign = jnp.where((out_lane % DIFF_HD) < half, -1.0, 1.0)
    sel_cos = sel.astype(BF16)
    sel_sin = (sel * sign[None, :]).astype(BF16)
    seq_spec = pl.BlockSpec((1, s, LANES), lambda i: (i, 0, 0))
    return pl.pallas_call(
        _rope_kernel,
        grid=(b,),
        in_specs=[
            pl.BlockSpec((1, rows, LANES), lambda i: (i, 0, 0)),
            pl.BlockSpec((1, LANES), lambda i: (0, 0)),
            pl.BlockSpec((LANES, ROPE_FOLD * LANES), lambda i: (0, 0)),
            pl.BlockSpec((LANES, ROPE_FOLD * LANES), lambda i: (0, 0)),
        ],
        out_specs=[seq_spec, seq_spec],
        out_shape=[jax.ShapeDtypeStruct((b, s, LANES), F32)] * 2,
        compiler_params=pltpu.CompilerParams(dimension_semantics=("parallel",)),
        name="rope_tables",
    )(pos_c, invf, sel_cos, sel_sin)


def _mix_in_kernel(
        x_ref, cos_ref, sin_ref, gmix_ref,
        win_ref, wdtt_ref,
        cw_ref, cb_ref, lng_ref, lnb_ref,
        gq_ref, gk_ref, g32_ref,
        sw_ref, sb_ref, dtbt_ref, alogt_ref, dskip_ref, ng_ref,
        triu_ref, exp_ref,
        ycs_ref, q_ref, k_ref, v_ref,
        hn_scr, uc_scr, q_scr, k_scr, z_scr, xbc_scr, dtt_scr, hbuf, xbuf, h_scr,
        *, tile):
    n_chunks = tile // ROW_CHUNK
    d_model = x_ref.shape[-1]
    edges = [0]
    for width in (2 * CONV_CH, DIFF_W, DIFF_W, DIFF_W, SSD_W, SSD_XBC):
        edges.append(edges[-1] + width)
    col = [slice(edges[j], edges[j + 1]) for j in range(6)]

    @pl.when(pl.program_id(1) == 0)
    def _():
        hbuf[0:CONV_HALO, :] = jnp.zeros((CONV_HALO, CONV_CH), F32)
        xbuf[0:SMALL_HALO, :] = jnp.zeros((SMALL_HALO, SSD_XBC), F32)
        h_scr[...] = jnp.zeros(h_scr.shape, F32)

    xr = x_ref[0]
    ms = jnp.sum(xr * xr, axis=-1, keepdims=True) * (1.0 / d_model)
    hn_scr[...] = (xr * lax.rsqrt(ms + EPS) * gmix_ref[...]).astype(BF16)

    hn = hn_scr[...]
    uc_scr[...] = _dot(hn, win_ref[:, col[0]])
    q_scr[...] = _dot(hn, win_ref[:, col[1]])
    k_scr[...] = _dot(hn, win_ref[:, col[2]])
    v_ref[0] = _dot(hn, win_ref[:, col[3]]).astype(BF16)
    z_scr[...] = _dot(hn, win_ref[:, col[4]])
    xbc_scr[...] = _dot(hn, win_ref[:, col[5]])
    dtt = _dot_nt(wdtt_ref[...], hn)
    for c in range(n_chunks):
        dtt_scr[c] = dtt[:, c * ROW_CHUNK:(c + 1) * ROW_CHUNK]

    lane128 = lax.broadcasted_iota(jnp.int32, (1, LANES), 1)
    lane256 = lax.broadcasted_iota(jnp.int32, (1, 2 * LANES), 1)
    first_half = (lane256 % DIFF_HD) < (DIFF_HD // 2)
    row_i = lax.broadcasted_iota(jnp.int32, (ROW_CHUNK, ROW_CHUNK), 0)
    col_i = lax.broadcasted_iota(jnp.int32, (ROW_CHUNK, ROW_CHUNK), 1)
    causal = col_i <= row_i
    a_col = -jnp.exp(alogt_ref[...])

    def norm_rope(t, g_ref, cos, sin):
        ms = _dot((t * t).astype(BF16), g32_ref[...])
        tn = t * lax.rsqrt(ms + EPS) * g_ref[...]
        fwd = pltpu.roll(tn, 2 * LANES - DIFF_HD // 2, axis=1)
        bwd = pltpu.roll(tn, DIFF_HD // 2, axis=1)
        rot = jnp.where(first_half, fwd, bwd)
        lo = tn[:, :LANES] * cos + rot[:, :LANES] * sin
        hi = tn[:, LANES:] * cos + rot[:, LANES:] * sin
        return jnp.concatenate([lo, hi], axis=1).astype(BF16)

    def mix_rows(c):
        r0 = c * ROW_CHUNK
        rows = pl.ds(r0, ROW_CHUNK)

        uc = uc_scr[rows, :]
        hbuf[pl.ds(r0 + CONV_HALO, ROW_CHUNK), :] = uc[:, :CONV_CH] * _sigmoid(uc[:, CONV_CH:])
        win = hbuf[pl.ds(r0, ROW_CHUNK + CONV_HALO), :]
        hc = _causal_dwconv_rows(win, CONV_HALO, CONV_WIDTH, cw_ref, cb_ref[...], ROW_CHUNK)
        mu = jnp.mean(hc, axis=-1, keepdims=True)
        xc = hc - mu
        var = jnp.mean(xc * xc, axis=-1, keepdims=True)
        yln = xc * lax.rsqrt(var + EPS) * lng_ref[...] + lnb_ref[...]
        ycs_ref[0, rows, 0:CONV_CH] = _silu(yln).astype(BF16)

        cos = cos_ref[0, rows, :]
        sin = sin_ref[0, rows, :]
        q_ref[0, rows, :] = norm_rope(q_scr[rows, :], gq_ref, cos, sin)
        k_ref[0, rows, :] = norm_rope(k_scr[rows, :], gk_ref, cos, sin)

        xbuf[pl.ds(r0 + SMALL_HALO, ROW_CHUNK), :] = xbc_scr[rows, :]
        xwin = xbuf[pl.ds(r0, ROW_CHUNK + SMALL_HALO), :]
        xbc = _silu(_causal_dwconv_rows(xwin, SMALL_HALO, SSD_CONV, sw_ref, sb_ref[...], ROW_CHUNK))
        xs = xbc[:, :SSD_W]
        bmat = xbc[:, SSD_W:SSD_W + SSD_GROUPS * SSD_STATE]
        cmat = xbc[:, SSD_W + SSD_GROUPS * SSD_STATE:]

        dt_t = _softplus(dtt_scr[c] + dtbt_ref[...])
        cs_t = _dot_exact_lhs(dt_t * a_col, triu_ref[...], 3)
        packed = jnp.concatenate(
            [dt_t[0:SSD_HEADS], cs_t[0:SSD_HEADS],
             jnp.zeros((ROW_CHUNK - 2 * SSD_HEADS, ROW_CHUNK), F32)], axis=0)
        cols_t = packed.T
        cs = pltpu.roll(cols_t, LANES - SSD_HEADS, axis=1)

        dt_w = _dot_exact_lhs(cols_t, exp_ref[...], 2)
        cs_w = _dot_exact_lhs(cs, exp_ref[...], 2)
        cs_last = cs_w[ROW_CHUNK - 1:ROW_CHUNK, :]
        xdt = xs * dt_w
        xdt_b = xdt.astype(BF16)
        xdec_b = (xdt * jnp.exp(cs_last - cs_w)).astype(BF16)
        ecs = jnp.exp(cs_w)
        chunk_decay = jnp.exp(cs_last)
        z = z_scr[rows, :]

        gw = SSD_W // SSD_GROUPS
        for g in range(SSD_GROUPS):
            bg = bmat[:, g * SSD_STATE:(g + 1) * SSD_STATE]
            cg_b = cmat[:, g * SSD_STATE:(g + 1) * SSD_STATE].astype(BF16)
            cb = _dot_nt(cg_b, bg.astype(BF16))
            yd_parts = []
            for pr in range(gw // LANES):
                xpair = xdt_b[:, g * gw + pr * LANES:g * gw + (pr + 1) * LANES]
                outs = []
                for hh in range(LANES // SSD_HD):
                    h = (g * gw + pr * LANES) // SSD_HD + hh
                    seg = cs[:, h:h + 1] - cs_t[h:h + 1, :]
                    lmat = jnp.exp(jnp.where(causal, seg, -jnp.inf))
                    outs.append(_dot((cb * lmat).astype(BF16), xpair))
                yd_parts.append(jnp.where(lane128 < SSD_HD, outs[0], outs[1]))
            yd = jnp.concatenate(yd_parts, axis=1)
            hstate = h_scr[g]
            yo = _dot(cg_b, hstate.astype(BF16)) * ecs[:, g * gw:(g + 1) * gw]
            st = _dot(bg.T.astype(BF16), xdec_b[:, g * gw:(g + 1) * gw])
            h_scr[g] = hstate * chunk_decay[:, g * gw:(g + 1) * gw] + st
            yg = (yd + yo + xs[:, g * gw:(g + 1) * gw] * dskip_ref[:, g * gw:(g + 1) * gw])
            yg = yg * _silu(z[:, g * gw:(g + 1) * gw])
            ms = jnp.mean(yg * yg, axis=-1, keepdims=True)
            yn = yg * lax.rsqrt(ms + EPS) * ng_ref[:, g * gw:(g + 1) * gw]
            ycs_ref[0, rows, CONV_CH + g * gw:CONV_CH + (g + 1) * gw] = yn.astype(BF16)

    for c in range(n_chunks):
        mix_rows(c)

    hbuf[0:CONV_HALO, :] = hbuf[tile:tile + CONV_HALO, :]
    xbuf[0:SMALL_HALO, :] = xbuf[tile:tile + SMALL_HALO, :]


def _const_spec(shape):
    nd = len(shape)
    return pl.BlockSpec(shape, lambda *_: (0,) * nd, pipeline_mode=pl.Buffered(1))


def _layer_spec(block, index):
    return pl.BlockSpec((None,) + tuple(block), lambda *_: tuple(index), pipeline_mode=pl.Buffered(1))


def _mix_in(x, cos_t, sin_t, layer, wts, lp, consts, tile):
    b, s, d = x.shape
    proj_cols = 2 * CONV_CH + 3 * DIFF_W + SSD_W + SSD_XBC
    small = [
        lp["w_dtt"],
        lp["conv_w"], lp["conv_b"], lp["conv_ln_g"], lp["conv_ln_b"],
        lp["gq"], lp["gk"], consts["g32"],
        lp["ssd_w"], lp["ssd_b"], lp["dt_bias_t"], lp["a_log_t"],
        lp["d_skip"], lp["ssd_norm_g"],
        consts["triu"], consts["expand"],
    ]
    seq_map = lambda i, j: (i, j, 0)
    in_specs = [
        pl.BlockSpec((1, tile, d), seq_map),
        pl.BlockSpec((1, tile, LANES), seq_map),
        pl.BlockSpec((1, tile, LANES), seq_map),
        _const_spec(lp["g_mix"].shape),
        _layer_spec((d, proj_cols), (layer, 0, 0)),
    ] + [_const_spec(a.shape) for a in small]
    out_shape = [
        jax.ShapeDtypeStruct((b, s, CONV_CH + SSD_W), BF16),
        jax.ShapeDtypeStruct((b, s, DIFF_W), BF16),
        jax.ShapeDtypeStruct((b, s, DIFF_W), BF16),
        jax.ShapeDtypeStruct((b, s, DIFF_W), BF16),
    ]
    out_specs = [
        pl.BlockSpec((1, tile, CONV_CH + SSD_W), seq_map),
        pl.BlockSpec((1, tile, DIFF_W), seq_map),
        pl.BlockSpec((1, tile, DIFF_W), seq_map),
        pl.BlockSpec((1, tile, DIFF_W), seq_map),
    ]
    scratch = [
        pltpu.VMEM((tile, d), BF16),
        pltpu.VMEM((tile, 2 * CONV_CH), F32),
        pltpu.VMEM((tile, DIFF_W), F32),
        pltpu.VMEM((tile, DIFF_W), F32),
        pltpu.VMEM((tile, SSD_W), F32),
        pltpu.VMEM((tile, SSD_XBC), F32),
        pltpu.VMEM((tile // ROW_CHUNK, 2 * SUBLANES, ROW_CHUNK), F32),
        pltpu.VMEM((tile + CONV_HALO, CONV_CH), F32),
        pltpu.VMEM((tile + SMALL_HALO, SSD_XBC), F32),
        pltpu.VMEM((SSD_GROUPS, SSD_STATE, SSD_W // SSD_GROUPS), F32),
    ]
    return pl.pallas_call(
        functools.partial(_mix_in_kernel, tile=tile),
        grid=(b, s // tile),
        in_specs=in_specs,
        out_specs=out_specs,
        out_shape=out_shape,
        scratch_shapes=scratch,
        compiler_params=pltpu.CompilerParams(
            dimension_semantics=("parallel", "arbitrary"), vmem_limit_bytes=VMEM_LIMIT_BYTES),
        name="mix_in",
    )(x, cos_t, sin_t, lp["g_mix"], wts["w_in"], *small)


def _attn_kernel(lamv_ref, subg_ref, q_ref, k_ref, v_ref, o_ref, vext, q4_scr, acc_scr, m_scr, sa, sb,
                 *, blk, lam_init):
    i = pl.program_id(2)
    n_heads = LANES // DIFF_VD
    n_maps = 2 * n_heads
    rows = n_maps * blk
    lane = lax.broadcasted_iota(jnp.int32, (1, LANES), 1)
    lo_half = lane < DIFF_VD

    @pl.when(i == 0)
    def _():
        vb = v_ref[0]
        one = jnp.ones_like(vb)
        vext[0] = jnp.where(lo_half, vb, one)
        vext[1] = jnp.where(lo_half, one, vb)

    qb = q_ref[0]
    for j in range(n_maps):
        keep = (lane >= j * DIFF_HD) & (lane < (j + 1) * DIFF_HD)
        q4_scr[j * blk:(j + 1) * blk, :] = jnp.where(keep, qb, jnp.zeros_like(qb))
    m_scr[...] = jnp.full((rows, LANES), -jnp.inf, F32)
    acc_scr[...] = jnp.zeros((rows, LANES), F32)

    row_i = lax.broadcasted_iota(jnp.int32, (blk, blk), 0)
    col_i = lax.broadcasted_iota(jnp.int32, (blk, blk), 1)
    causal = col_i <= row_i

    def scores(kb, s_ref):
        k0 = pl.multiple_of(kb * blk, blk)
        s_ref[...] = _dot_nt(q4_scr[...], k_ref[0, pl.ds(k0, blk), :])

    def consume(kb, s_ref, masked):
        k0 = pl.multiple_of(kb * blk, blk)
        s = s_ref[...]
        if masked:
            s = jnp.concatenate(
                [jnp.where(causal, s[j * blk:(j + 1) * blk, :], -jnp.inf) for j in range(n_maps)], axis=0)
        cols = [s[:, c * LANES:(c + 1) * LANES] for c in range(blk // LANES)]
        rm = cols[0]
        for cc in cols[1:]:
            rm = jnp.maximum(rm, cc)
        m_old = m_scr[...]
        m_new = jnp.maximum(m_old, jnp.max(rm, axis=-1, keepdims=True))
        alpha = jnp.exp2(m_old - m_new)
        p = jnp.concatenate([jnp.exp2(cc - m_new) for cc in cols], axis=1).astype(BF16)
        hr = rows // n_heads
        pv = jnp.concatenate(
            [_dot(p[h * hr:(h + 1) * hr, :], vext[h, pl.ds(k0, blk), :]) for h in range(n_heads)], axis=0)
        acc_scr[...] = acc_scr[...] * alpha + pv
        m_scr[...] = m_new

    scores(0, sa)

    def pair(t, carry):
        scores(2 * t + 1, sb)
        consume(2 * t, sa, False)
        scores(2 * t + 2, sa)
        consume(2 * t + 1, sb, False)
        return carry

    lax.fori_loop(0, i // 2, pair, 0)

    @pl.when(i % 2 == 0)
    def _():
        consume(i, sa, True)

    @pl.when(i % 2 == 1)
    def _():
        scores(i, sb)
        consume(i - 1, sa, False)
        consume(i, sb, True)

    lv = lamv_ref[...]
    lam = (jnp.exp(jnp.sum(lv[0:1] * lv[1:2], axis=-1, keepdims=True))
           - jnp.exp(jnp.sum(lv[2:3] * lv[3:4], axis=-1, keepdims=True)) + lam_init)
    outs = []
    for h in range(n_heads):
        a1 = acc_scr[(2 * h) * blk:(2 * h + 1) * blk, :]
        a2 = acc_scr[(2 * h + 1) * blk:(2 * h + 2) * blk, :]
        r1 = a1 / pltpu.roll(a1, DIFF_VD, axis=1)
        r2 = a2 / pltpu.roll(a2, DIFF_VD, axis=1)
        outs.append(r1 - lam * r2)
    o = jnp.where(lo_half, outs[0], outs[1])
    o2 = o * o
    ms_lo = jnp.sum(jnp.where(lo_half, o2, 0.0), axis=-1, keepdims=True)
    ms_hi = jnp.sum(jnp.where(lo_half, 0.0, o2), axis=-1, keepdims=True)
    ms = jnp.where(lo_half, ms_lo, ms_hi) * (1.0 / DIFF_VD)
    o_ref[0] = (o * lax.rsqrt(ms + EPS) * subg_ref[...]).astype(BF16)


def _diff_attn(q, k, v, lamv, subg, lam_init, blk):
    b, s, w = q.shape
    n_pairs = w // LANES
    n_heads = LANES // DIFF_VD
    return pl.pallas_call(
        functools.partial(_attn_kernel, blk=blk, lam_init=lam_init),
        grid=(b, n_pairs, s // blk),
        in_specs=[
            pl.BlockSpec((4, LANES), lambda bi, pi, i: (0, 0)),
            pl.BlockSpec((1, LANES), lambda bi, pi, i: (0, 0)),
            pl.BlockSpec((1, blk, LANES), lambda bi, pi, i: (bi, i, pi)),
            pl.BlockSpec((1, s, LANES), lambda bi, pi, i: (bi, 0, pi)),
            pl.BlockSpec((1, s, LANES), lambda bi, pi, i: (bi, 0, pi)),
        ],
        out_specs=pl.BlockSpec((1, blk, LANES), lambda bi, pi, i: (bi, i, pi)),
        out_shape=jax.ShapeDtypeStruct((b, s, w), BF16),
        scratch_shapes=[
            pltpu.VMEM((n_heads, s, LANES), BF16),
            pltpu.VMEM((2 * n_heads * blk, LANES), BF16),
            pltpu.VMEM((2 * n_heads * blk, LANES), F32),
            pltpu.VMEM((2 * n_heads * blk, LANES), F32),
            pltpu.VMEM((2 * n_heads * blk, blk), F32),
            pltpu.VMEM((2 * n_heads * blk, blk), F32),
        ],
        compiler_params=pltpu.CompilerParams(
            dimension_semantics=("parallel", "parallel", "arbitrary"), vmem_limit_bytes=VMEM_LIMIT_BYTES),
        name="diff_attn",
    )(lamv, subg, q, k, v)


def _mix_out_kernel(
        x_ref, ycs_ref, yd_ref, p_ref,
        wo_c_ref, wo_d_ref, wo_s_ref, gffn_ref, wg_ref, wv_ref, fw_ref, fb_ref, wdown_ref,
        gple_ref, wpg_ref, wple_ref,
        o_ref,
        h_scr, gate_buf,
        *, tile):
    d_model = x_ref.shape[-1]
    d_ff = wdown_ref.shape[0]

    @pl.when(pl.program_id(1) == 0)
    def _():
        gate_buf[0:SMALL_HALO, :] = jnp.zeros((SMALL_HALO, d_ff), F32)

    def rms(v, g_ref):
        ms = jnp.sum(v * v, axis=-1, keepdims=True) * (1.0 / d_model)
        return (v * lax.rsqrt(ms + EPS) * g_ref[...]).astype(BF16)

    x1 = (x_ref[0] + _dot(ycs_ref[0, :, 0:CONV_CH], wo_c_ref[...]) + _dot(yd_ref[0], wo_d_ref[...])
          + _dot(ycs_ref[0, :, CONV_CH:], wo_s_ref[...]))
    o_ref[0] = x1
    h_scr[...] = rms(x1, gffn_ref)

    hval = h_scr[...]
    gate_buf[SMALL_HALO:SMALL_HALO + tile, :] = _dot(hval, wg_ref[...])
    val = _dot(hval, wv_ref[...])
    base = SMALL_HALO - (FFN_CONV - 1)
    gate = fb_ref[...]
    for k in range(FFN_CONV):
        gate = gate + fw_ref[k:k + 1, :] * gate_buf[base + k:base + k + tile, :]
    hid = (_silu(gate) * val).astype(BF16)
    o_ref[0] = o_ref[0] + _dot(hid, wdown_ref[...])
    gate_buf[0:SMALL_HALO, :] = gate_buf[tile:tile + SMALL_HALO, :]

    x2 = o_ref[0]
    gate = _sigmoid(_dot(rms(x2, gple_ref), wpg_ref[...]))
    o_ref[0] = x2 + _dot(p_ref[0].astype(BF16), wple_ref[...]) * gate


def _mix_out(x, ycs, ydiff, p, layer, wts, lp, tile):
    b, s, d = x.shape
    d_ff = wts["w_down"].shape[1]
    d_ple = p.shape[-1]
    seq_map = lambda i, j: (i, j, 0)
    in_specs = [
        pl.BlockSpec((1, tile, d), seq_map),
        pl.BlockSpec((1, tile, ycs.shape[-1]), seq_map),
        pl.BlockSpec((1, tile, ydiff.shape[-1]), seq_map),
        pl.BlockSpec((None, 1, tile, d_ple), lambda i, j: (layer, i, j, 0)),
        _layer_spec((CONV_CH, d), (layer, 0, 0)),
        _layer_spec((DIFF_W, d), (layer, CONV_CH // DIFF_W, 0)),
        _layer_spec((SSD_W, d), (layer, (CONV_CH + DIFF_W) // SSD_W, 0)),
        _const_spec(lp["g_ffn"].shape),
        _layer_spec((d, d_ff), (layer, 0, 0)),
        _layer_spec((d, d_ff), (layer, 0, 1)),
        _const_spec(lp["ffn_w"].shape), _const_spec(lp["ffn_b"].shape),
        _layer_spec((d_ff, d), (layer, 0, 0)),
        _const_spec(lp["g_ple"].shape),
        _layer_spec((d, d), (layer, 0, 0)),
        _layer_spec((d_ple, d), (layer, 0, 0)),
    ]
    return pl.pallas_call(
        functools.partial(_mix_out_kernel, tile=tile),
        grid=(b, s // tile),
        in_specs=in_specs,
        out_specs=pl.BlockSpec((1, tile, d), seq_map),
        out_shape=jax.ShapeDtypeStruct((b, s, d), F32),
        scratch_shapes=[
            pltpu.VMEM((tile, d), BF16),
            pltpu.VMEM((tile + SMALL_HALO, d_ff), F32),
        ],
        compiler_params=pltpu.CompilerParams(
            dimension_semantics=("parallel", "arbitrary"), vmem_limit_bytes=VMEM_LIMIT_BYTES),
        name="mix_out",
    )(x, ycs, ydiff, p, wts["w_out"], wts["w_out"], wts["w_out"], lp["g_ffn"], wts["w_up"], wts["w_up"],
      lp["ffn_w"], lp["ffn_b"], wts["w_down"], lp["g_ple"], wts["w_ple_gate"], wts["w_ple"])


def _pad_rows(a, rows):
    return jnp.pad(a, ((0, rows - a.shape[0]), (0, 0)))


def _pad_lanes(a, lanes):
    return jnp.pad(a, ((0, 0), (0, lanes - a.shape[1])))


def _constants():
    r = jnp.arange(ROW_CHUNK)
    triu = (r[:, None] <= r[None, :]).astype(BF16)
    lane = jnp.arange(LANES)
    col = jnp.arange(SSD_W)
    expand = (lane[:, None] == (col[None, :] // SSD_HD)).astype(BF16)
    c2 = jnp.arange(DIFF_W)
    g32 = ((c2[:, None] // DIFF_HD) == (c2[None, :] // DIFF_HD)).astype(F32) / DIFF_HD
    return {"triu": triu, "expand": expand, "g32": g32.astype(BF16)}


def _layer_params(i, w):
    proj_cols = 2 * CONV_CH + 3 * DIFF_W + SSD_W + SSD_XBC
    w_dt = w["w_in"][i][:, proj_cols:]
    row = lambda v: v[i][None, :].astype(F32)
    n_rep = DIFF_W // DIFF_HD
    lam_init = 0.8 - 0.6 * math.exp(-0.3 * i)
    return {
        "g_mix": row(w["g_mix"]),
        "w_dtt": _pad_rows(w_dt.T, 2 * SUBLANES).astype(BF16),
        "conv_w": _pad_rows(w["conv_dw_w"][i], 32).astype(F32),
        "conv_b": row(w["conv_dw_b"]), "conv_ln_g": row(w["conv_ln_g"]), "conv_ln_b": row(w["conv_ln_b"]),
        "gq": jnp.tile(w["q_norm_g"][i].astype(F32), n_rep)[None, :] * (DIFF_HD ** -0.5 * LOG2_E),
        "gk": jnp.tile(w["k_norm_g"][i].astype(F32), n_rep)[None, :],
        "ssd_w": _pad_rows(w["ssd_conv_w"][i], SUBLANES).astype(F32),
        "ssd_b": row(w["ssd_conv_b"]),
        "dt_bias_t": jnp.broadcast_to(_pad_rows(w["ssd_dt_bias"][i][:, None].astype(F32), 2 * SUBLANES),
                                      (2 * SUBLANES, ROW_CHUNK)),
        "a_log_t": jnp.broadcast_to(_pad_rows(w["ssd_a_log"][i][:, None].astype(F32), 2 * SUBLANES),
                                    (2 * SUBLANES, ROW_CHUNK)),
        "d_skip": jnp.repeat(w["ssd_d"][i].astype(F32), SSD_HD)[None, :],
        "ssd_norm_g": row(w["ssd_norm_g"]),
        "lamv": _pad_lanes(jnp.stack([w["lam_q1"][i], w["lam_k1"][i], w["lam_q2"][i], w["lam_k2"][i]]).astype(F32),
                           LANES),
        "subg": jnp.tile(w["attn_subln_g"][i].astype(F32), LANES // DIFF_VD)[None, :] * (1.0 - lam_init),
        "lam_init": lam_init,
        "g_ffn": row(w["g_ffn"]),
        "ffn_w": _pad_rows(w["ffn_dw_w"][i], SUBLANES).astype(F32), "ffn_b": row(w["ffn_dw_b"]),
        "g_ple": row(w["g_ple"]),
    }


def kernel(x, p, positions, g_mix, w_in, conv_dw_w, conv_dw_b, conv_ln_g, conv_ln_b, q_norm_g, k_norm_g, lam_q1, lam_k1, lam_q2, lam_k2, attn_subln_g, ssd_conv_w, ssd_conv_b, ssd_dt_bias, ssd_a_log, ssd_d, ssd_norm_g, w_out, g_ffn, w_up, ffn_dw_w, ffn_dw_b, w_down, g_ple, w_ple_gate, w_ple):
    w = dict(g_mix=g_mix, w_in=w_in, conv_dw_w=conv_dw_w, conv_dw_b=conv_dw_b, conv_ln_g=conv_ln_g,
             conv_ln_b=conv_ln_b, q_norm_g=q_norm_g, k_norm_g=k_norm_g, lam_q1=lam_q1, lam_k1=lam_k1,
             lam_q2=lam_q2, lam_k2=lam_k2, attn_subln_g=attn_subln_g, ssd_conv_w=ssd_conv_w,
             ssd_conv_b=ssd_conv_b, ssd_dt_bias=ssd_dt_bias, ssd_a_log=ssd_a_log, ssd_d=ssd_d,
             ssd_norm_g=ssd_norm_g, g_ffn=g_ffn, ffn_dw_w=ffn_dw_w, ffn_dw_b=ffn_dw_b, g_ple=g_ple)
    wts = {"w_in": w_in.astype(BF16), "w_out": w_out.astype(BF16), "w_up": w_up.astype(BF16),
           "w_down": w_down.astype(BF16), "w_ple_gate": w_ple_gate.astype(BF16), "w_ple": w_ple.astype(BF16)}
    b, s, d = x.shape
    depth = w_in.shape[0]
    tile = min(s, 512)
    blk = min(s, 512)
    assert s % tile == 0 and tile % ROW_CHUNK == 0 and s % blk == 0 and blk % LANES == 0
    assert w_up.shape[-1] % (2 * LANES) == 0 and CONV_CH == DIFF_W and (CONV_CH + DIFF_W) == SSD_W
    consts = _constants()
    cos_t, sin_t = _rope_tables(positions)
    for i in range(depth):
        lp = _layer_params(i, w)
        ycs, q, k, v = _mix_in(x, cos_t, sin_t, i, wts, lp, consts, tile)
        ydiff = _diff_attn(q, k, v, lp["lamv"], lp["subg"], lp["lam_init"], blk)
        x = _mix_out(x, ycs, ydiff, p, i, wts, lp, tile)
    return x
```
